```python
import math
import jax
import jax.numpy as jnp
from jax import lax
import numpy as np

D_MODEL = 2048
BATCH = 1
SEQ = 16384
DEPTH = 1

MIX_WIDTH = D_MODEL
ATTN_WIDTH = MIX_WIDTH // 2
SSM_WIDTH = MIX_WIDTH - ATTN_WIDTH
HEAD_DIM = 64
N_HEADS = ATTN_WIDTH // HEAD_DIM
ATTN_BRANCHES = ((128, 1), (512, 4), (2048, 16))
SSM_GROUP_CH = 16
N_SSM_GROUPS = SSM_WIDTH // SSM_GROUP_CH
SSM_STATE = 64
DT_MIN = 0.001
DT_MAX = 0.1
IN_WIDTH = 3 * ATTN_WIDTH + SSM_WIDTH
N_EXPERTS = 32
TOP_K = 4
D_FF_EXPERT = D_MODEL
SWIGLU_LIMIT = 7.0
SWIGLU_ALPHA = 1.702
EXPERT_BLOCK = 128
NORM_EPS = 1e-6

kernel_name = 'hymba_s5_longnet_moe_block'


def rmsnorm(x, g):
    xf = x.astype(jnp.float32)
    y = xf * lax.rsqrt(jnp.mean(xf * xf, axis=-1, keepdims=True) + NORM_EPS)
    return (y * g.astype(jnp.float32)).astype(x.dtype)


def dilated_branch(q, k, v, window, dil):
    Bb, S, H, Dh = q.shape
    blk = window // dil
    span = blk * dil
    nb = -(-S // span)
    Sp = nb * span
    padw = ((0, 0), (0, Sp - S), (0, 0), (0, 0))

    def split_stride(t):
        return jnp.pad(t, padw).reshape(Bb, nb, blk, dil, H, Dh)

    def with_prev(t):
        prev = jnp.pad(t[:, :-1], ((0, 0), (1, 0), (0, 0), (0, 0), (0, 0), (0, 0)))
        return jnp.concatenate([prev, t], axis=2)

    qb = split_stride(q)
    kc = with_prev(split_stride(k))
    vc = with_prev(split_stride(v))
    s = jnp.einsum('bnqrhd,bnkrhd->bnrhqk', qb, kc,
                   preferred_element_type=jnp.float32) * (Dh ** -0.5)
    qi = jnp.arange(blk)[:, None]
    kj = jnp.arange(2 * blk)[None, :]
    dist = qi + blk - kj
    band = (dist >= 0) & (dist <= blk)
    first = (jnp.arange(nb)[:, None, None] > 0) | (kj[None] >= blk)
    mask = (band[None] & first)[None, :, None, None]
    s = jnp.where(mask, s, -jnp.inf)
    m = jnp.max(s, axis=-1, keepdims=True)
    p = jnp.exp(s - m)
    den = jnp.sum(p, axis=-1, keepdims=True)
    o = jnp.einsum('bnrhqk,bnkrhd->bnrhqd', p, vc.astype(jnp.float32)) / den
    lse = (m + jnp.log(den))[..., 0]
    o = jnp.transpose(o, (0, 1, 4, 2, 3, 5)).reshape(Bb, Sp, H, Dh)[:, :S]
    lse = jnp.transpose(lse, (0, 1, 4, 2, 3)).reshape(Bb, Sp, H)[:, :S]
    return o, lse


def dilated_mixture_attention(q, k, v):
    outs = []
    lses = []
    for window, dil in ATTN_BRANCHES:
        o, lse = dilated_branch(q, k, v, window, dil)
        outs.append(o)
        lses.append(lse)
    wts = jax.nn.softmax(jnp.stack(lses, axis=0), axis=0)[..., None]
    o = jnp.sum(wts * jnp.stack(outs, axis=0), axis=0)
    return o.astype(q.dtype)


def s5_mixer(u, lam_re, lam_im, b_re, b_im, c_re, c_im, d_skip, log_dt, w_glu, b_glu):
    Bb, S, W = u.shape
    f32 = jnp.float32
    uf = u.astype(f32).reshape(Bb, S, N_SSM_GROUPS, SSM_GROUP_CH)
    dt = jnp.exp(log_dt.astype(f32))[:, None]
    lr = lam_re.astype(f32)
    li = lam_im.astype(f32)
    mag = jnp.exp(lr * dt)
    ab_re = mag * jnp.cos(li * dt)
    ab_im = mag * jnp.sin(li * dt)
    den = lr * lr + li * li
    nr = ab_re - 1.0
    ni = ab_im
    f_re = (nr * lr + ni * li) / den
    f_im = (ni * lr - nr * li) / den
    br = b_re.astype(f32)
    bi = b_im.astype(f32)
    bb_re = f_re[..., None] * br - f_im[..., None] * bi
    bb_im = f_re[..., None] * bi + f_im[..., None] * br
    bu_re = jnp.einsum('gnp,bsgp->bsgn', bb_re, uf)
    bu_im = jnp.einsum('gnp,bsgp->bsgn', bb_im, uf)
    a_re = jnp.broadcast_to(ab_re, bu_re.shape)
    a_im = jnp.broadcast_to(ab_im, bu_im.shape)

    def combine(e1, e2):
        a1r, a1i, b1r, b1i = e1
        a2r, a2i, b2r, b2i = e2
        return (a2r * a1r - a2i * a1i,
                a2r * a1i + a2i * a1r,
                a2r * b1r - a2i * b1i + b2r,
                a2r * b1i + a2i * b1r + b2i)

    _, _, xr, xi = lax.associative_scan(combine, (a_re, a_im, bu_re, bu_im), axis=1)
    y = (jnp.einsum('gpn,bsgn->bsgp', c_re.astype(f32), xr)
         - jnp.einsum('gpn,bsgn->bsgp', c_im.astype(f32), xi)
         + d_skip.astype(f32) * uf)
    y = jax.nn.gelu(y.reshape(Bb, S, W))
    y = y * jax.nn.sigmoid(y @ w_glu.astype(f32) + b_glu.astype(f32))
    return y.astype(u.dtype)


def clamped_swiglu(gu):
    x_glu = jnp.minimum(gu[..., ::2], SWIGLU_LIMIT)
    x_lin = jnp.clip(gu[..., 1::2], -SWIGLU_LIMIT, SWIGLU_LIMIT)
    return x_glu * jax.nn.sigmoid(SWIGLU_ALPHA * x_glu) * (x_lin + 1.0)


def moe_ffn(h, l, w_router, b_router, w_gate_up, b_gate_up, w_down, b_down):
    Bb, S, D = h.shape
    T = Bb * S
    hf = h.reshape(T, D)
    logits = (hf @ w_router[l] + b_router[l]).astype(jnp.float32)
    top_vals, top_idx = lax.top_k(logits, TOP_K)
    gates = jax.nn.softmax(top_vals, axis=-1)
    n_assign = T * TOP_K
    flat_e = top_idx.reshape(-1).astype(jnp.int32)
    flat_tok = jnp.arange(n_assign, dtype=jnp.int32) // TOP_K
    flat_gate = gates.reshape(-1)
    order = jnp.argsort(flat_e)
    sorted_e = flat_e[order]
    sorted_tok = flat_tok[order]
    sorted_gate = flat_gate[order]
    counts = jnp.bincount(flat_e, length=N_EXPERTS).astype(jnp.int32)
    start = jnp.cumsum(counts) - counts
    padded = (counts + EXPERT_BLOCK - 1) // EXPERT_BLOCK * EXPERT_BLOCK
    pad_end = jnp.cumsum(padded)
    pad_start = pad_end - padded
    rank = jnp.arange(n_assign, dtype=jnp.int32) - start[sorted_e]
    dest = pad_start[sorted_e] + rank
    m_pad = n_assign + N_EXPERTS * EXPERT_BLOCK
    n_blocks = m_pad // EXPERT_BLOCK
    row_tok = jnp.full((m_pad,), T, jnp.int32).at[dest].set(sorted_tok)
    row_gate = jnp.zeros((m_pad,), jnp.float32).at[dest].set(sorted_gate)
    block_e = jnp.minimum(
        jnp.searchsorted(pad_end, jnp.arange(n_blocks, dtype=jnp.int32) * EXPERT_BLOCK,
                         side='right'), N_EXPERTS - 1).astype(jnp.int32)
    h_pad = jnp.concatenate([hf, jnp.zeros((1, D), hf.dtype)], axis=0)

    def expert_block(args):
        e, tok, g = args
        xb = h_pad[tok]
        gu = xb @ w_gate_up[l, e] + b_gate_up[l, e]
        yb = clamped_swiglu(gu) @ w_down[l, e] + b_down[l, e]
        return yb * g[:, None].astype(yb.dtype)

    out = lax.map(expert_block, (block_e,
                                 row_tok.reshape(n_blocks, EXPERT_BLOCK),
                                 row_gate.reshape(n_blocks, EXPERT_BLOCK)))
    y = jax.ops.segment_sum(out.reshape(m_pad, D), row_tok, num_segments=T + 1)[:T]
    return y.reshape(Bb, S, D).astype(h.dtype)


def setup_inputs(seed: int = 0) -> dict:
    key = jax.random.key(seed)
    ks = jax.random.split(key, 30)
    f32 = jnp.float32
    L, D, A, W = DEPTH, D_MODEL, ATTN_WIDTH, SSM_WIDTH
    G, N, P = N_SSM_GROUPS, SSM_STATE, SSM_GROUP_CH
    E, F = N_EXPERTS, D_FF_EXPERT

    def nrm(k, shape, scale):
        return jax.random.normal(k, shape, f32) * scale

    n_idx = jnp.arange(N, dtype=f32)[None, None, :]
    return {
        'x': nrm(ks[0], (BATCH, SEQ, D), 1.0),
        'c': nrm(ks[1], (BATCH, D), 1.0),
        'w_ada': nrm(ks[2], (L, D, 6 * D), D ** -0.5),
        'b_ada': nrm(ks[3], (L, 6 * D), 0.02),
        'norm1_g': 1.0 + nrm(ks[4], (L, D), 0.02),
        'w_in': nrm(ks[5], (L, D, IN_WIDTH), D ** -0.5),
        'lambda_re': -0.5 + nrm(ks[6], (L, G, N), 0.01),
        'lambda_im': jnp.pi * n_idx + nrm(ks[7], (L, G, N), 0.01),
        'ssm_b_re': nrm(ks[8], (L, G, N, P), (2 * P) ** -0.5),
        'ssm_b_im': nrm(ks[9], (L, G, N, P), (2 * P) ** -0.5),
        'ssm_c_re': nrm(ks[10], (L, G, P, N), (2 * N) ** -0.5),
        'ssm_c_im': nrm(ks[11], (L, G, P, N), (2 * N) ** -0.5),
        'ssm_d': nrm(ks[12], (L, G, P), 1.0),
        'ssm_log_dt': jax.random.uniform(ks[13], (L, G), f32, math.log(DT_MIN), math.log(DT_MAX)),
        'w_glu': nrm(ks[14], (L, W, W), W ** -0.5),
        'b_glu': nrm(ks[15], (L, W), 0.02),
        'attn_out_g': 1.0 + nrm(ks[16], (L, A), 0.02),
        'ssm_out_g': 1.0 + nrm(ks[17], (L, W), 0.02),
        'w_out': nrm(ks[18], (L, MIX_WIDTH, D), MIX_WIDTH ** -0.5),
        'norm2_g': 1.0 + nrm(ks[19], (L, D), 0.02),
        'w_router': nrm(ks[20], (L, D, E), D ** -0.5),
        'b_router': nrm(ks[21], (L, E), 0.01),
        'w_gate_up': nrm(ks[22], (L, E, D, 2 * F), D ** -0.5),
        'b_gate_up': nrm(ks[23], (L, E, 2 * F), 0.02),
        'w_down': nrm(ks[24], (L, E, F, D), F ** -0.5),
        'b_down': nrm(ks[25], (L, E, D), 0.02),
        'final_g': 1.0 + nrm(ks[26], (D,), 0.02),
    }


def reference(x, c, w_ada, b_ada, norm1_g, w_in, lambda_re, lambda_im, ssm_b_re, ssm_b_im,
              ssm_c_re, ssm_c_im, ssm_d, ssm_log_dt, w_glu, b_glu, attn_out_g, ssm_out_g,
              w_out, norm2_g, w_router, b_router, w_gate_up, b_gate_up, w_down, b_down,
              final_g):
    Bb, S, _ = x.shape
    c_act = jax.nn.silu(c)
    for l in range(DEPTH):
        mod = c_act @ w_ada[l] + b_ada[l]
        sh1, sc1, g1, sh2, sc2, g2 = [m[:, None, :] for m in jnp.split(mod, 6, axis=-1)]
        h = rmsnorm(x, norm1_g[l]) * (1.0 + sc1) + sh1
        proj = h @ w_in[l]
        q, k, v, u = jnp.split(proj, [ATTN_WIDTH, 2 * ATTN_WIDTH, 3 * ATTN_WIDTH], axis=-1)
        q = q.reshape(Bb, S, N_HEADS, HEAD_DIM)
        k = k.reshape(Bb, S, N_HEADS, HEAD_DIM)
        v = v.reshape(Bb, S, N_HEADS, HEAD_DIM)
        y_attn = dilated_mixture_attention(q, k, v).reshape(Bb, S, ATTN_WIDTH)
        y_ssm = s5_mixer(u, lambda_re[l], lambda_im[l], ssm_b_re[l], ssm_b_im[l],
                         ssm_c_re[l], ssm_c_im[l], ssm_d[l], ssm_log_dt[l], w_glu[l], b_glu[l])
        y_mix = jnp.concatenate([rmsnorm(y_attn, attn_out_g[l]),
                                 rmsnorm(y_ssm, ssm_out_g[l])], axis=-1)
        x = x + g1 * (y_mix @ w_out[l])
        h2 = rmsnorm(x, norm2_g[l]) * (1.0 + sc2) + sh2
        x = x + g2 * moe_ffn(h2, l, w_router, b_router, w_gate_up, b_gate_up, w_down, b_down)
    return rmsnorm(x, final_g)
```

```python
import functools
import math

import jax
import jax.numpy as jnp
from jax import lax
from jax.experimental import pallas as pl
from jax.experimental.pallas import tpu as pltpu

F32 = jnp.float32
BF16 = jnp.bfloat16
I32 = jnp.int32

HEAD_DIM = 64
ATTN_BLOCK = 128
ATTN_DILATIONS = (1, 4, 16)
SSM_P = 16
SSM_N = 64
SSM_L = 16
SSM_SEG = 8
TOP_K = 4
SWIGLU_LIMIT = 7.0
SWIGLU_ALPHA = 1.702
NORM_EPS = 1e-6
LANES = 128
NEG_BIG = -1e30
VMEM_LIMIT = 56 * 1024 * 1024


def _cparams(n_axes, vmem=VMEM_LIMIT):
    return pltpu.CompilerParams(dimension_semantics=("arbitrary",) * n_axes,
                                vmem_limit_bytes=vmem)


def _rms(x, g):
    return x * lax.rsqrt(jnp.mean(x * x, axis=-1, keepdims=True) + NORM_EPS) * g


def _adaln_body(c_ref, w_ref, b_ref, o_ref):
    c = c_ref[...]
    ca = c * jax.nn.sigmoid(c)
    o_ref[...] = jnp.sum(ca * w_ref[...], axis=0, keepdims=True) + b_ref[...]


def _adaln(c, w, b):
    D, N = w.shape
    bn = next(b for b in (1024, 512, 256, LANES) if N % b == 0)
    return pl.pallas_call(
        _adaln_body,
        grid=(N // bn,),
        in_specs=[pl.BlockSpec((D, 1), lambda j: (0, 0)),
                  pl.BlockSpec((D, bn), lambda j: (0, j)),
                  pl.BlockSpec((1, bn), lambda j: (0, j))],
        out_specs=pl.BlockSpec((1, bn), lambda j: (0, j)),
        out_shape=jax.ShapeDtypeStruct((1, N), F32),
        compiler_params=_cparams(1),
        name="adaln",
    )(c.reshape(D, 1), w, b.reshape(1, N))


def _inproj_body(x_ref, g_ref, mod_ref, w_ref, o_ref):
    h = _rms(x_ref[...], g_ref[...]) * (1.0 + mod_ref[1:2, :]) + mod_ref[0:1, :]
    o_ref[...] = jnp.dot(h.astype(BF16), w_ref[...], preferred_element_type=F32).astype(BF16)


def _inproj(x, g, mod, w_bf16, tm, tn):
    S, D = x.shape
    N = w_bf16.shape[1]
    return pl.pallas_call(
        _inproj_body,
        grid=(N // tn, S // tm),
        in_specs=[pl.BlockSpec((tm, D), lambda j, i: (i, 0)),
                  pl.BlockSpec((1, D), lambda j, i: (0, 0)),
                  pl.BlockSpec((6, D), lambda j, i: (0, 0)),
                  pl.BlockSpec((D, tn), lambda j, i: (0, j))],
        out_specs=pl.BlockSpec((tm, tn), lambda j, i: (i, j)),
        out_shape=jax.ShapeDtypeStruct((S, N), BF16),
        compiler_params=_cparams(2),
        name="inproj",
    )(x, g.reshape(1, D), mod, w_bf16)


def _attn_body(q_ref, kp_ref, kc_ref, vp_ref, vc_ref, o_ref, lse_ref, *, n_heads):
    n = pl.program_id(1)
    B = ATTN_BLOCK
    row = lax.broadcasted_iota(I32, (B, 2 * B), 0)
    col = lax.broadcasted_iota(I32, (B, 2 * B), 1)
    mask = (col >= row) & (col <= row + B) & ((col >= B) | (n > 0))
    lane = lax.broadcasted_iota(I32, (B, LANES), 1)
    lse_tile = jnp.zeros((B, LANES), F32)
    scale = HEAD_DIM ** -0.5
    for h in range(n_heads):
        hs = slice(h * HEAD_DIM, (h + 1) * HEAD_DIM)
        q = q_ref[:, hs]
        k = jnp.concatenate([kp_ref[:, hs], kc_ref[:, hs]], axis=0)
        v = jnp.concatenate([vp_ref[:, hs], vc_ref[:, hs]], axis=0)
        s = lax.dot_general(q, k, (((1,), (1,)), ((), ())), preferred_element_type=F32) * scale
        s = jnp.where(mask, s, NEG_BIG)
        m = jnp.max(s, axis=-1, keepdims=True)
        p = jnp.exp(s - m)
        l = jnp.sum(p, axis=-1, keepdims=True)
        o = jnp.dot(p.astype(BF16), v, preferred_element_type=F32) / l
        o_ref[:, hs] = o.astype(BF16)
        lse_tile = jnp.where(lane == h, m + jnp.log(l), lse_tile)
    lse_ref[...] = lse_tile


def _attn_branch(proj, dil, A):
    S = proj.shape[0]
    rows = S // dil
    nb = rows // ATTN_BLOCK
    pv = proj.reshape(rows, dil * 4 * A)
    blk = (ATTN_BLOCK, A)
    cur = lambda c: (lambda r, n: (n, r * 4 + c))
    prev = lambda c: (lambda r, n: (jnp.maximum(n - 1, 0), r * 4 + c))
    o, lse = pl.pallas_call(
        functools.partial(_attn_body, n_heads=A // HEAD_DIM),
        grid=(dil, nb),
        in_specs=[pl.BlockSpec(blk, cur(0)),
                  pl.BlockSpec(blk, prev(1)), pl.BlockSpec(blk, cur(1)),
                  pl.BlockSpec(blk, prev(2)), pl.BlockSpec(blk, cur(2))],
        out_specs=[pl.BlockSpec(blk, lambda r, n: (n, r)),
                   pl.BlockSpec((ATTN_BLOCK, LANES), lambda r, n: (n, r))],
        out_shape=[jax.ShapeDtypeStruct((rows, dil * A), BF16),
                   jax.ShapeDtypeStruct((rows, dil * LANES), F32)],
        compiler_params=_cparams(2),
        name=f"attn_d{dil}",
    )(pv, pv, pv, pv, pv)
    return o.reshape(S, A), lse.reshape(S, LANES)


def _ssm_tables(lam_re, lam_im, b_re, b_im, c_re, c_im, d_skip, log_dt, seg_len):
    hp = lax.Precision.HIGHEST
    G, N = lam_re.shape
    P, L = SSM_P, SSM_L
    dt = jnp.exp(log_dt)[:, None]
    lr, li = lam_re, lam_im

    def powers(j):
        j = j[..., None, None]
        mag = jnp.exp(j * (lr * dt))
        return mag * jnp.cos(j * (li * dt)), mag * jnp.sin(j * (li * dt))

    pr, pi = powers(jnp.arange(L + 1, dtype=F32))
    den = lr * lr + li * li
    nr, ni = pr[1] - 1.0, pi[1]
    f_re = (nr * lr + ni * li) / den
    f_im = (ni * lr - nr * li) / den
    bb_re = f_re[..., None] * b_re - f_im[..., None] * b_im
    bb_im = f_re[..., None] * b_im + f_im[..., None] * b_re
    ca_re = c_re[None] * pr[:, :, None, :] - c_im[None] * pi[:, :, None, :]
    ca_im = c_re[None] * pi[:, :, None, :] + c_im[None] * pr[:, :, None, :]
    kk = (jnp.einsum('jgpn,gnq->jgpq', ca_re[:L], bb_re, precision=hp)
          - jnp.einsum('jgpn,gnq->jgpq', ca_im[:L], bb_im, precision=hp))
    kk = kk.at[0].add(d_skip[:, :, None] * jnp.eye(P, dtype=F32)[None])
    lag = jnp.arange(L)[None, :] - jnp.arange(L)[:, None]
    tt = jnp.where((lag >= 0)[:, :, None, None, None], kk[jnp.clip(lag, 0, L - 1)], 0.0)
    tt = jnp.transpose(tt, (2, 0, 4, 1, 3)).reshape(G, L * P, L * P)
    wr = pr[L - 1::-1][:L, :, :, None] * bb_re[None] - pi[L - 1::-1][:L, :, :, None] * bb_im[None]
    wi = pr[L - 1::-1][:L, :, :, None] * bb_im[None] + pi[L - 1::-1][:L, :, :, None] * bb_re[None]
    wr = jnp.transpose(wr, (1, 0, 3, 2)).reshape(G, L * P, N)
    wi = jnp.transpose(wi, (1, 0, 3, 2)).reshape(G, L * P, N)
    vr = jnp.transpose(ca_re[1:], (1, 3, 0, 2)).reshape(G, N, L * P)
    vi = -jnp.transpose(ca_im[1:], (1, 3, 0, 2)).reshape(G, N, L * P)

    def pair_diag(m):
        g2 = G // 2
        m = m.reshape(g2, 2, m.shape[1], m.shape[2])
        z = jnp.zeros_like(m[:, 0])
        top = jnp.concatenate([m[:, 0], z], axis=2)
        bot = jnp.concatenate([z, m[:, 1]], axis=2)
        return jnp.concatenate([top, bot], axis=1)

    al_r, al_i = pr[L], pi[L]
    as_r, as_i = powers(jnp.asarray(float(L * seg_len), F32))
    avec = jnp.stack([al_r, al_i, as_r, as_i], axis=0).reshape(4, G * N)
    return (tt.astype(BF16), pair_diag(wr).astype(BF16), pair_diag(wi).astype(BF16),
            pair_diag(vr).astype(BF16), pair_diag(vi).astype(BF16), avec)


def _gelu_tanh(x):
    return 0.5 * x * (1.0 + jnp.tanh(math.sqrt(2.0 / math.pi) * (x + 0.044715 * (x * x * x))))


def _ssm_body(ut_ref, t_ref, wr_ref, wi_ref, vr_ref, vi_ref, a_ref, y_ref,
              er_ref, ei_ref, xr_ref, xi_ref, sr_ref, si_ref, *, pairs, seg_len):
    LP = SSM_L * SSM_P
    N2 = 2 * SSM_N
    for p in range(pairs):
        u = ut_ref[p]
        er_ref[:, p * N2:(p + 1) * N2] = jnp.dot(u, wr_ref[p], preferred_element_type=F32)
        ei_ref[:, p * N2:(p + 1) * N2] = jnp.dot(u, wi_ref[p], preferred_element_type=F32)
    ar, ai = a_ref[0:1, :], a_ref[1:2, :]
    asr, asi = a_ref[2:3, :], a_ref[3:4, :]
    W = pairs * N2

    def slab(j):
        return pl.ds(pl.multiple_of(j * SSM_SEG, SSM_SEG), SSM_SEG)

    def end_state(j, carry):
        zr, zi = carry
        return (ar * zr - ai * zi + er_ref[slab(j), :], ar * zi + ai * zr + ei_ref[slab(j), :])

    zr, zi = lax.fori_loop(0, seg_len, end_state,
                           (jnp.zeros((SSM_SEG, W), F32), jnp.zeros((SSM_SEG, W), F32)))
    cr = jnp.zeros((1, W), F32)
    ci = jnp.zeros((1, W), F32)
    for s in range(SSM_SEG):
        sr_ref[s:s + 1, :] = cr
        si_ref[s:s + 1, :] = ci
        cr, ci = (asr * cr - asi * ci + zr[s:s + 1, :], asr * ci + asi * cr + zi[s:s + 1, :])

    def prefix(j, carry):
        xr, xi = carry
        xr_ref[slab(j), :] = xr
        xi_ref[slab(j), :] = xi
        return (ar * xr - ai * xi + er_ref[slab(j), :], ar * xi + ai * xr + ei_ref[slab(j), :])

    lax.fori_loop(0, seg_len, prefix, (sr_ref[...], si_ref[...]))
    for p in range(pairs):
        u = ut_ref[p]
        y0 = jnp.dot(u[:, :LP], t_ref[2 * p], preferred_element_type=F32)
        y1 = jnp.dot(u[:, LP:], t_ref[2 * p + 1], preferred_element_type=F32)
        y = jnp.concatenate([y0, y1], axis=1)
        y = y + jnp.dot(xr_ref[:, p * N2:(p + 1) * N2].astype(BF16), vr_ref[p], preferred_element_type=F32)
        y = y + jnp.dot(xi_ref[:, p * N2:(p + 1) * N2].astype(BF16), vi_ref[p], preferred_element_type=F32)
        y_ref[p] = _gelu_tanh(y).astype(BF16)


def _ssm(u, tables, pairs):
    S, Wd = u.shape
    G = Wd // SSM_P
    L, P, N = SSM_L, SSM_P, SSM_N
    NC = S // L
    seg_len = NC // SSM_SEG
    tt, wr, wi, vr, vi, avec = tables
    g2 = G // 2
    LP2 = 2 * L * P
    ut = u.reshape(SSM_SEG, seg_len, L, g2, 2, P).transpose(3, 1, 0, 4, 2, 5).reshape(g2, NC, LP2)
    nsteps = g2 // pairs
    W = pairs * 2 * N
    avec = avec.reshape(4, nsteps, W).transpose(1, 0, 2)
    yt = pl.pallas_call(
        functools.partial(_ssm_body, pairs=pairs, seg_len=seg_len),
        grid=(nsteps,),
        in_specs=[pl.BlockSpec((pairs, NC, LP2), lambda i: (i, 0, 0)),
                  pl.BlockSpec((2 * pairs, L * P, L * P), lambda i: (i, 0, 0)),
                  pl.BlockSpec((pairs, LP2, 2 * N), lambda i: (i, 0, 0)),
                  pl.BlockSpec((pairs, LP2, 2 * N), lambda i: (i, 0, 0)),
                  pl.BlockSpec((pairs, 2 * N, LP2), lambda i: (i, 0, 0)),
                  pl.BlockSpec((pairs, 2 * N, LP2), lambda i: (i, 0, 0)),
                  pl.BlockSpec((None, 4, W), lambda i: (i, 0, 0))],
        out_specs=pl.BlockSpec((pairs, NC, LP2), lambda i: (i, 0, 0)),
        out_shape=jax.ShapeDtypeStruct((g2, NC, LP2), BF16),
        scratch_shapes=[pltpu.VMEM((NC, W), F32)] * 4 + [pltpu.VMEM((SSM_SEG, W), F32)] * 2,
        compiler_params=_cparams(1),
        name="ssm",
    )(ut, tt, wr, wi, vr, vi, avec)
    y = yt.reshape(g2, seg_len, SSM_SEG, 2, L, P).transpose(2, 1, 4, 0, 3, 5).reshape(S, Wd)
    return y


def _mixout_body(x_ref, o1_ref, o2_ref, o3_ref, l1_ref, l2_ref, l3_ref, ys_ref,
                 wglu_ref, bglu_ref, ag_ref, sg_ref, wout_ref, mod_ref, n2g_ref, wr_ref, br_ref,
                 x1_ref, h2_ref, ri_ref, rg_ref, cnt_ref, carry_ref, *, n_heads):
    i = pl.program_id(0)
    TM = x_ref.shape[0]

    @pl.when(i == 0)
    def _():
        carry_ref[...] = jnp.zeros_like(carry_ref)

    l1, l2, l3 = l1_ref[...], l2_ref[...], l3_ref[...]
    lm = jnp.maximum(jnp.maximum(l1, l2), l3)
    e1, e2, e3 = jnp.exp(l1 - lm), jnp.exp(l2 - lm), jnp.exp(l3 - lm)
    inv = 1.0 / (e1 + e2 + e3)
    w1, w2, w3 = e1 * inv, e2 * inv, e3 * inv
    pieces = []
    for h in range(n_heads):
        hs = slice(h * HEAD_DIM, (h + 1) * HEAD_DIM)
        pieces.append(w1[:, h:h + 1] * o1_ref[:, hs].astype(F32)
                      + w2[:, h:h + 1] * o2_ref[:, hs].astype(F32)
                      + w3[:, h:h + 1] * o3_ref[:, hs].astype(F32))
    ya = _rms(jnp.concatenate(pieces, axis=1), ag_ref[...])
    ys = ys_ref[...]
    z = jnp.dot(ys, wglu_ref[...], preferred_element_type=F32) + bglu_ref[...]
    yg = _rms(ys.astype(F32) * jax.nn.sigmoid(z), sg_ref[...])
    ymix = jnp.concatenate([ya, yg], axis=1).astype(BF16)
    x1 = x_ref[...] + mod_ref[2:3, :] * jnp.dot(ymix, wout_ref[...], preferred_element_type=F32)
    x1_ref[...] = x1
    h2 = _rms(x1, n2g_ref[...]) * (1.0 + mod_ref[4:5, :]) + mod_ref[3:4, :]
    h2_ref[...] = h2
    logits = jnp.dot(h2.astype(BF16), wr_ref[...], preferred_element_type=F32) + br_ref[...]
    lane = lax.broadcasted_iota(I32, (TM, LANES), 1).astype(F32)
    work = logits
    vals, idxs = [], []
    for _ in range(TOP_K):
        m = jnp.max(work, axis=1, keepdims=True)
        idx = jnp.min(jnp.where(work == m, lane, float(LANES)), axis=1, keepdims=True)
        vals.append(m)
        idxs.append(idx)
        work = jnp.where(lane == idx, -3e38, work)
    ex = [jnp.exp(v - vals[0]) for v in vals]
    den = ex[0] + ex[1] + ex[2] + ex[3]
    onehot = jnp.zeros((TM, LANES), F32)
    for idx in idxs:
        onehot = onehot + jnp.where(lane == idx, 1.0, 0.0)
    r_i = lax.broadcasted_iota(I32, (TM, TM), 0)
    c_i = lax.broadcasted_iota(I32, (TM, TM), 1)
    tri = jnp.where(c_i < r_i, 1.0, 0.0).astype(BF16)
    before = jnp.dot(tri, onehot.astype(BF16), preferred_element_type=F32) + carry_ref[...]
    ri = jnp.zeros((TM, LANES), F32)
    rg = jnp.zeros((TM, LANES), F32)
    for k in range(TOP_K):
        rank = jnp.sum(jnp.where(lane == idxs[k], before, 0.0), axis=1, keepdims=True)
        ri = jnp.where(lane == float(k), idxs[k], ri)
        ri = jnp.where(lane == float(TOP_K + k), rank, ri)
        rg = jnp.where(lane == float(k), ex[k] / den, rg)
    ri_ref[...] = ri.astype(I32)
    rg_ref[...] = rg
    carry_ref[...] = carry_ref[...] + jnp.sum(onehot, axis=0, keepdims=True)
    cnt_ref[...] = carry_ref[...]


def _mixout(x, o, lse, ys, wglu, bglu, ag, sg, wout, mod, n2g, wr_pad, br_pad, tm):
    S, D = x.shape
    A = o[0].shape[1]
    Wd = ys.shape[1]
    row = lambda w: pl.BlockSpec((tm, w), lambda i: (i, 0))
    full = lambda a: pl.BlockSpec(a.shape, lambda i: (0,) * a.ndim)
    ins = [x, o[0], o[1], o[2], lse[0], lse[1], lse[2], ys,
           wglu, bglu, ag, sg, wout, mod, n2g, wr_pad, br_pad]
    in_specs = [row(D), row(A), row(A), row(A), row(LANES), row(LANES), row(LANES), row(Wd)]
    in_specs += [full(a) for a in ins[8:]]
    return pl.pallas_call(
        functools.partial(_mixout_body, n_heads=A // HEAD_DIM),
        grid=(S // tm,),
        in_specs=in_specs,
        out_specs=[row(D), row(D), row(LANES), row(LANES), pl.BlockSpec((1, LANES), lambda i: (0, 0))],
        out_shape=[jax.ShapeDtypeStruct((S, D), F32), jax.ShapeDtypeStruct((S, D), F32),
                   jax.ShapeDtypeStruct((S, LANES), I32), jax.ShapeDtypeStruct((S, LANES), F32),
                   jax.ShapeDtypeStruct((1, LANES), F32)],
        scratch_shapes=[pltpu.VMEM((1, LANES), F32)],
        compiler_params=_cparams(1),
        name="mixout",
    )(*ins)


def _ffn_body(ie_ref, ns_ref, tok_hbm, h_hbm, wgu_ref, bgu_ref, wd_ref, bd_ref, sel_ref, o_ref,
              idx_smem, xf_ref, xb_ref, sem, *, ts, sub, nf):
    i = pl.program_id(0)
    j = pl.program_id(1)
    nsub = ns_ref[i]

    @pl.when(j == 0)
    def _():
        o_ref[...] = jnp.zeros_like(o_ref)

    @pl.when((nsub > 0) & (j == 0))
    def _():
        cp = pltpu.make_async_copy(tok_hbm.at[pl.ds(pl.multiple_of(i * ts, ts), ts)], idx_smem, sem.at[0])
        cp.start()
        cp.wait()
        nrows = nsub * sub

        def issue(r, c):
            pltpu.make_async_copy(h_hbm.at[idx_smem[r]], xf_ref.at[r], sem.at[1]).start()
            return c

        lax.fori_loop(0, nrows, issue, 0)

        def drain(r, c):
            pltpu.make_async_copy(h_hbm.at[0], xf_ref.at[r], sem.at[1]).wait()
            return c

        lax.fori_loop(0, nrows, drain, 0)

        def cast(s, c):
            rows = pl.ds(pl.multiple_of(s * sub, sub), sub)
            xb_ref[rows, :] = xf_ref[rows, :].astype(BF16)
            return c

        lax.fori_loop(0, nsub, cast, 0)

    @pl.when(nsub > 0)
    def _():
        wgu = wgu_ref[...].astype(BF16)
        wd = wd_ref[...].astype(BF16)
        bgu = bgu_ref[...]
        n2 = wgu.shape[1]

        def block(s, c):
            rows = pl.ds(pl.multiple_of(s * sub, sub), sub)
            gu = jnp.dot(xb_ref[rows, :], wgu, preferred_element_type=F32) + bgu
            glu = jnp.minimum(gu, SWIGLU_LIMIT)
            a = glu * jax.nn.sigmoid(SWIGLU_ALPHA * glu)
            b = jnp.clip(gu, -SWIGLU_LIMIT, SWIGLU_LIMIT) + 1.0
            prod = (a * pltpu.roll(b, n2 - 1, axis=1)).astype(BF16)
            act = jnp.dot(prod, sel_ref[...], preferred_element_type=F32).astype(BF16)
            o_ref[rows, :] = o_ref[rows, :] + jnp.dot(act, wd, preferred_element_type=F32)
            return c

        lax.fori_loop(0, nsub, block, 0)

    @pl.when((nsub > 0) & (j == nf - 1))
    def _():
        o_ref[...] = o_ref[...] + bd_ref[...]


def _ffn(item_e, item_nsub, row_tok, h2, wgu, bgu, wd, bd, ts, sub, fc):
    E, D, F2 = wgu.shape
    F = F2 // 2
    nf = F // fc
    n_items = item_e.shape[0]
    sel = (jnp.arange(2 * fc)[:, None] == 2 * jnp.arange(fc)[None, :]).astype(BF16)

    def chunk(i, j, ie, ns):
        return jnp.where(ns[i] > 0, j, nf - 1)

    grid_spec = pltpu.PrefetchScalarGridSpec(
        num_scalar_prefetch=2,
        grid=(n_items, nf),
        in_specs=[pl.BlockSpec(memory_space=pl.ANY),
                  pl.BlockSpec(memory_space=pl.ANY),
                  pl.BlockSpec((None, D, 2 * fc), lambda i, j, ie, ns: (ie[i], 0, chunk(i, j, ie, ns))),
                  pl.BlockSpec((None, 1, 2 * fc), lambda i, j, ie, ns: (ie[i], 0, chunk(i, j, ie, ns))),
                  pl.BlockSpec((None, fc, D), lambda i, j, ie, ns: (ie[i], chunk(i, j, ie, ns), 0)),
                  pl.BlockSpec((None, 1, D), lambda i, j, ie, ns: (ie[i], 0, 0)),
                  pl.BlockSpec((2 * fc, fc), lambda i, j, ie, ns: (0, 0))],
        out_specs=pl.BlockSpec((ts, D), lambda i, j, ie, ns: (i, 0)),
        scratch_shapes=[pltpu.SMEM((ts,), I32),
                        pltpu.VMEM((ts, D), F32),
                        pltpu.VMEM((ts, D), BF16),
                        pltpu.SemaphoreType.DMA((2,))],
    )
    return pl.pallas_call(
        functools.partial(_ffn_body, ts=ts, sub=sub, nf=nf),
        grid_spec=grid_spec,
        out_shape=jax.ShapeDtypeStruct((n_items * ts, D), F32),
        compiler_params=_cparams(2),
        name="ffn",
    )(item_e, item_nsub, row_tok, h2, wgu, bgu.reshape(E, 1, F2), wd, bd.reshape(E, 1, D), sel)


def _combine_body(dest_hbm, ys_hbm, x1_ref, rg_ref, mod_ref, fg_ref, o_ref, idx_smem, buf_ref, sem, *, tm):
    i = pl.program_id(0)
    n = TOP_K * tm
    cp = pltpu.make_async_copy(dest_hbm.at[pl.ds(pl.multiple_of(i * n, n), n)], idx_smem, sem.at[0])
    cp.start()
    cp.wait()

    def issue(r, c):
        pltpu.make_async_copy(ys_hbm.at[idx_smem[r]], buf_ref.at[r], sem.at[1]).start()
        return c

    lax.fori_loop(0, n, issue, 0)

    def drain(r, c):
        pltpu.make_async_copy(ys_hbm.at[0], buf_ref.at[r], sem.at[1]).wait()
        return c

    lax.fori_loop(0, n, drain, 0)
    rg = rg_ref[...]
    y = jnp.zeros(x1_ref.shape, F32)
    for k in range(TOP_K):
        y = y + rg[:, k:k + 1] * buf_ref[k * tm:(k + 1) * tm, :]
    x2 = x1_ref[...] + mod_ref[5:6, :] * y
    o_ref[...] = _rms(x2, fg_ref[...])


def _combine(dest_km, y_sorted, x1, rg, mod, fg, tm):
    S, D = x1.shape
    return pl.pallas_call(
        functools.partial(_combine_body, tm=tm),
        grid=(S // tm,),
        in_specs=[pl.BlockSpec(memory_space=pl.ANY),
                  pl.BlockSpec(memory_space=pl.ANY),
                  pl.BlockSpec((tm, D), lambda i: (i, 0)),
                  pl.BlockSpec((tm, LANES), lambda i: (i, 0)),
                  pl.BlockSpec((6, D), lambda i: (0, 0)),
                  pl.BlockSpec((1, D), lambda i: (0, 0))],
        out_specs=pl.BlockSpec((tm, D), lambda i: (i, 0)),
        out_shape=jax.ShapeDtypeStruct((S, D), F32),
        scratch_shapes=[pltpu.SMEM((TOP_K * tm,), I32),
                        pltpu.VMEM((TOP_K * tm, D), F32),
                        pltpu.SemaphoreType.DMA((2,))],
        compiler_params=_cparams(1),
        name="combine",
    )(dest_km, y_sorted, x1, rg, mod, fg.reshape(1, D))


def _routing_tables(ri, counts_f, n_experts, ts, sub, tm_c):
    S = ri.shape[0]
    counts = counts_f[0, :n_experts].astype(I32)
    nblk = (counts + ts - 1) // ts
    blk_end = jnp.cumsum(nblk)
    blk_start = blk_end - nblk
    e_idx = ri[:, :TOP_K]
    rank = ri[:, TOP_K:2 * TOP_K]
    dest = blk_start[e_idx] * ts + rank
    max_items = (S * TOP_K) // ts + n_experts
    tok = jnp.broadcast_to(jnp.arange(S, dtype=I32)[:, None], (S, TOP_K))
    row_tok = jnp.zeros((max_items * ts,), I32).at[dest.reshape(-1)].set(tok.reshape(-1))
    item = jnp.arange(max_items, dtype=I32)
    n_items = blk_end[-1]
    item_c = jnp.minimum(item, n_items - 1)
    item_e = jnp.minimum(jnp.searchsorted(blk_end, item_c, side='right'), n_experts - 1).astype(I32)
    rows_left = counts[item_e] - (item_c - blk_start[item_e]) * ts
    nsub = jnp.clip((rows_left + sub - 1) // sub, 0, ts // sub)
    item_nsub = jnp.where(item < n_items, nsub, 0).astype(I32)
    dest_km = dest.reshape(S // tm_c, tm_c, TOP_K).transpose(0, 2, 1).reshape(-1)
    return item_e, item_nsub, row_tok, dest_km


def _forward(x, c, w_ada, b_ada, norm1_g, w_in, lambda_re, lambda_im, ssm_b_re, ssm_b_im,
             ssm_c_re, ssm_c_im, ssm_d, ssm_log_dt, w_glu, b_glu, attn_out_g, ssm_out_g,
             w_out, norm2_g, w_router, b_router, w_gate_up, b_gate_up, w_down, b_down, final_g,
             *, tiles):
    B, S, D = x.shape
    assert B == 1 and w_ada.shape[0] == 1
    A = attn_out_g.shape[1]
    Wd = ssm_out_g.shape[1]
    assert A == Wd and w_in.shape[2] == 3 * A + Wd
    E = w_router.shape[2]
    xs = x.reshape(S, D)

    mod = _adaln(c, w_ada[0], b_ada[0]).reshape(6, D)
    proj = _inproj(xs, norm1_g[0], mod, w_in[0].astype(BF16), tiles['tm_in'], A)
    branches = [_attn_branch(proj, d, A) for d in ATTN_DILATIONS]
    o = [b[0] for b in branches]
    lse = [b[1] for b in branches]

    seg_len = S // SSM_L // SSM_SEG
    tables = _ssm_tables(lambda_re[0], lambda_im[0], ssm_b_re[0], ssm_b_im[0], ssm_c_re[0],
                         ssm_c_im[0], ssm_d[0], ssm_log_dt[0], seg_len)
    ys = _ssm(proj[:, 3 * A:], tables, tiles['ssm_pairs'])

    wr_pad = jnp.zeros((D, LANES), BF16).at[:, :E].set(w_router[0].astype(BF16))
    br_pad = jnp.full((1, LANES), NEG_BIG, F32).at[0, :E].set(b_router[0])
    x1, h2, ri, rg, counts = _mixout(
        xs, o, lse, ys, w_glu[0].astype(BF16), b_glu[0].reshape(1, Wd), attn_out_g[0].reshape(1, A),
        ssm_out_g[0].reshape(1, Wd), w_out[0].astype(BF16), mod, norm2_g[0].reshape(1, D),
        wr_pad, br_pad, tiles['tm_mix'])

    ts, sub, fc, tm_c = tiles['ts'], tiles['sub'], tiles['fc'], tiles['tm_c']
    item_e, item_nsub, row_tok, dest_km = _routing_tables(ri, counts, E, ts, sub, tm_c)
    y_sorted = _ffn(item_e, item_nsub, row_tok, h2, w_gate_up[0], b_gate_up[0], w_down[0], b_down[0],
                    ts, sub, fc)
    out = _combine(dest_km, y_sorted, x1, rg, mod, final_g, tm_c)
    return out.reshape(B, S, D)


TILES = dict(tm_in=512, ssm_pairs=4, tm_mix=256, ts=1024, sub=256, fc=256, tm_c=256)


def kernel(x, c, w_ada, b_ada, norm1_g, w_in, lambda_re, lambda_im, ssm_b_re, ssm_b_im, ssm_c_re, ssm_c_im, ssm_d, ssm_log_dt, w_glu, b_glu, attn_out_g, ssm_out_g, w_out, norm2_g, w_router, b_router, w_gate_up, b_gate_up, w_down, b_down, final_g):
    return _forward(x, c, w_ada, b_ada, norm1_g, w_in, lambda_re, lambda_im, ssm_b_re, ssm_b_im,
                    ssm_c_re, ssm_c_im, ssm_d, ssm_log_dt, w_glu, b_glu, attn_out_g, ssm_out_g,
                    w_out, norm2_g, w_router, b_router, w_gate_up, b_gate_up, w_down, b_down, final_g,
                    tiles=TILES)
```

```python
import functools
import math

import jax
import jax.numpy as jnp
from jax import lax
from jax.experimental import pallas as pl
from jax.experimental.pallas import tpu as pltpu

F32 = jnp.float32
BF16 = jnp.bfloat16
I32 = jnp.int32

HEAD_DIM = 64
ATTN_BLOCK = 128
ATTN_DILATIONS = (1, 4, 16)
SSM_P = 16
SSM_N = 64
SSM_L = 16
TOP_K = 4
SWIGLU_LIMIT = 7.0
SWIGLU_ALPHA = 1.702
NORM_EPS = 1e-6
LANES = 128
SSM_OCT = LANES // SSM_P
NEG_BIG = -1e30
VMEM_LIMIT = 56 * 1024 * 1024


def _cparams(n_axes, vmem=VMEM_LIMIT):
    return pltpu.CompilerParams(dimension_semantics=("arbitrary",) * n_axes,
                                vmem_limit_bytes=vmem)


def _rms(x, g):
    return x * lax.rsqrt(jnp.mean(x * x, axis=-1, keepdims=True) + NORM_EPS) * g


def _pack_bf16_pair(lo, hi):
    lb = lax.bitcast_convert_type(lo.astype(BF16).astype(F32), jnp.uint32)
    hb = lax.bitcast_convert_type(hi.astype(BF16).astype(F32), jnp.uint32)
    return (lb >> 16) | (hb & jnp.uint32(0xFFFF0000))


def _unpack_bf16_pair(w):
    return (lax.bitcast_convert_type(w << 16, F32),
            lax.bitcast_convert_type(w & jnp.uint32(0xFFFF0000), F32))


def _adaln_body(c_ref, w_ref, b_ref, o_ref):
    c = c_ref[...]
    ca = c * jax.nn.sigmoid(c)
    o_ref[...] = jnp.sum(ca * w_ref[...], axis=0, keepdims=True) + b_ref[...]


def _adaln(c, w, b):
    D, N = w.shape
    bn = next(b for b in (1024, 512, 256, LANES) if N % b == 0)
    return pl.pallas_call(
        _adaln_body,
        grid=(N // bn,),
        in_specs=[pl.BlockSpec((D, 1), lambda j: (0, 0)),
                  pl.BlockSpec((D, bn), lambda j: (0, j)),
                  pl.BlockSpec((1, bn), lambda j: (0, j))],
        out_specs=pl.BlockSpec((1, bn), lambda j: (0, j)),
        out_shape=jax.ShapeDtypeStruct((1, N), F32),
        compiler_params=_cparams(1),
        name="adaln",
    )(c.reshape(D, 1), w, b.reshape(1, N))


def _inproj_body(x_ref, g_ref, mod_ref, w_ref, o_ref):
    h = _rms(x_ref[...], g_ref[...]) * (1.0 + mod_ref[1:2, :]) + mod_ref[0:1, :]
    o_ref[...] = jnp.dot(h.astype(BF16), w_ref[...], preferred_element_type=F32).astype(o_ref.dtype)


def _inproj(x, g, mod, w_bf16, tm, tn, out_dtype, name):
    S, D = x.shape
    N = w_bf16.shape[1]
    return pl.pallas_call(
        _inproj_body,
        grid=(N // tn, S // tm),
        in_specs=[pl.BlockSpec((tm, D), lambda j, i: (i, 0)),
                  pl.BlockSpec((1, D), lambda j, i: (0, 0)),
                  pl.BlockSpec((6, D), lambda j, i: (0, 0)),
                  pl.BlockSpec((D, tn), lambda j, i: (0, j))],
        out_specs=pl.BlockSpec((tm, tn), lambda j, i: (i, j)),
        out_shape=jax.ShapeDtypeStruct((S, N), out_dtype),
        compiler_params=_cparams(2),
        name=name,
    )(x, g.reshape(1, D), mod, w_bf16)


def _attn_body(q_ref, kp_ref, kc_ref, vp_ref, vc_ref, o_ref, lse_ref, *, n_heads):
    n = pl.program_id(1)
    B = ATTN_BLOCK
    row = lax.broadcasted_iota(I32, (B, 2 * B), 0)
    col = lax.broadcasted_iota(I32, (B, 2 * B), 1)
    mask = (col >= row) & (col <= row + B) & ((col >= B) | (n > 0))
    lane = lax.broadcasted_iota(I32, (B, LANES), 1)
    lse_tile = jnp.zeros((B, LANES), F32)
    scale = HEAD_DIM ** -0.5
    for h in range(n_heads):
        hs = slice(h * HEAD_DIM, (h + 1) * HEAD_DIM)
        q = q_ref[:, hs]
        k = jnp.concatenate([kp_ref[:, hs], kc_ref[:, hs]], axis=0)
        v = jnp.concatenate([vp_ref[:, hs], vc_ref[:, hs]], axis=0)
        s = lax.dot_general(q, k, (((1,), (1,)), ((), ())), preferred_element_type=F32) * scale
        s = jnp.where(mask, s, NEG_BIG)
        m = jnp.max(s, axis=-1, keepdims=True)
        p = jnp.exp(s - m)
        l = jnp.sum(p, axis=-1, keepdims=True)
        o = jnp.dot(p.astype(BF16), v, preferred_element_type=F32) / l
        o_ref[:, hs] = o.astype(BF16)
        lse_tile = jnp.where(lane == h, m + jnp.log(l), lse_tile)
    lse_ref[...] = lse_tile


def _attn_branch(proj, dil, A):
    S = proj.shape[0]
    rows = S // dil
    nb = rows // ATTN_BLOCK
    pv = proj.reshape(rows, dil * 3 * A)
    blk = (ATTN_BLOCK, A)
    cur = lambda c: (lambda r, n: (n, r * 3 + c))
    prev = lambda c: (lambda r, n: (jnp.maximum(n - 1, 0), r * 3 + c))
    o, lse = pl.pallas_call(
        functools.partial(_attn_body, n_heads=A // HEAD_DIM),
        grid=(dil, nb),
        in_specs=[pl.BlockSpec(blk, cur(0)),
                  pl.BlockSpec(blk, prev(1)), pl.BlockSpec(blk, cur(1)),
                  pl.BlockSpec(blk, prev(2)), pl.BlockSpec(blk, cur(2))],
        out_specs=[pl.BlockSpec(blk, lambda r, n: (n, r)),
                   pl.BlockSpec((ATTN_BLOCK, LANES), lambda r, n: (n, r))],
        out_shape=[jax.ShapeDtypeStruct((rows, dil * A), BF16),
                   jax.ShapeDtypeStruct((rows, dil * LANES), F32)],
        compiler_params=_cparams(2),
        name=f"attn_d{dil}",
    )(pv, pv, pv, pv, pv)
    return o.reshape(S, A), lse.reshape(S, LANES)


def _ssm_tables(lam_re, lam_im, b_re, b_im, c_re, c_im, d_skip, log_dt):
    hp = lax.Precision.HIGHEST
    G, N = lam_re.shape
    P, L, Q = SSM_P, SSM_L, SSM_OCT
    O = G // Q
    dt = jnp.exp(log_dt)[:, None]
    lr, li = lam_re, lam_im

    def powers(j):
        j = j.astype(F32)[:, None, None]
        mag = jnp.exp(j * (lr * dt))
        return mag * jnp.cos(j * (li * dt)), mag * jnp.sin(j * (li * dt))

    def c_times(pr, pi):
        return (c_re[None] * pr[:, :, None, :] - c_im[None] * pi[:, :, None, :],
                c_re[None] * pi[:, :, None, :] + c_im[None] * pr[:, :, None, :])

    one_r, one_i = powers(jnp.ones((1,), I32))
    den = lr * lr + li * li
    nr, ni = one_r[0] - 1.0, one_i[0]
    f_re = (nr * lr + ni * li) / den
    f_im = (ni * lr - nr * li) / den
    bb_re = f_re[..., None] * b_re - f_im[..., None] * b_im
    bb_im = f_re[..., None] * b_im + f_im[..., None] * b_re
    down = (L - 1) - jnp.arange(L, dtype=I32)
    dr, di = powers(down)
    cr_, ci_ = c_times(dr, di)
    kk = (jnp.einsum('jgpn,gnq->jgpq', cr_, bb_re, precision=hp)
          - jnp.einsum('jgpn,gnq->jgpq', ci_, bb_im, precision=hp))
    skip = d_skip[:, :, None] * jnp.eye(P, dtype=F32)[None]
    kk = kk + jnp.where((down == 0)[:, None, None, None], skip[None], 0.0)
    eye = jnp.eye(Q, dtype=F32)
    dd = jnp.einsum('ab,joapq->ojaqbp', eye, kk.reshape(L, O, Q, P, P)).reshape(O, L, LANES, LANES)
    left = jnp.concatenate([dd[:, 1:], jnp.zeros_like(dd[:, :1])], axis=1)
    e2rev = jnp.concatenate([left, dd], axis=-1).reshape(O, L * LANES, 2 * LANES)
    prr, pir = dr[:, :, :, None], di[:, :, :, None]
    w_r = (prr * bb_re[None] - pir * bb_im[None]).reshape(L, O, Q, N, P)
    w_i = (prr * bb_im[None] + pir * bb_re[None]).reshape(L, O, Q, N, P)
    wr = jnp.einsum('ab,soanq->osaqbn', eye, w_r).reshape(O, L * LANES, Q * N)
    wi = jnp.einsum('ab,soanq->osaqbn', eye, w_i).reshape(O, L * LANES, Q * N)
    ur, ui = powers(1 + jnp.arange(L, dtype=I32))
    vr_, vi_ = c_times(ur, ui)
    vr = jnp.einsum('ab,toapn->oantbp', eye, vr_.reshape(L, O, Q, P, N)).reshape(O, Q * N, L * LANES)
    vi = -jnp.einsum('ab,toapn->oantbp', eye, vi_.reshape(L, O, Q, P, N)).reshape(O, Q * N, L * LANES)
    pr_l, pi_l = ur[L - 1], ui[L - 1]
    avec = jnp.stack([pr_l.reshape(O, Q * N), pi_l.reshape(O, Q * N)], axis=1)
    return (e2rev.astype(BF16), wr.astype(BF16), wi.astype(BF16), vr.astype(BF16), vi.astype(BF16), avec)


def _gelu_tanh(x):
    return 0.5 * x * (1.0 + jnp.tanh(math.sqrt(2.0 / math.pi) * (x + 0.044715 * (x * x * x))))


def _ssm_body(u3_hbm, e2_ref, wr_ref, wi_ref, vr_ref, vi_ref, a_ref, y3_hbm,
              ubuf, ucat, er_ref, ei_ref, xr_ref, xi_ref, sem):
    o = pl.program_id(0)
    L = SSM_L
    NC, W = er_ref.shape
    cols = pl.ds(pl.multiple_of(o * LANES, LANES), LANES)
    loads = [pltpu.make_async_copy(u3_hbm.at[:, t, cols], ubuf.at[t], sem.at[0]) for t in range(L)]
    for cp in loads:
        cp.start()
    for cp in loads:
        cp.wait()
    for t in range(L):
        ucat[:, t * LANES:(t + 1) * LANES] = ubuf[t].astype(BF16)
    er_ref[...] = jnp.dot(ucat[...], wr_ref[...], preferred_element_type=F32)
    ei_ref[...] = jnp.dot(ucat[...], wi_ref[...], preferred_element_type=F32)
    ar, ai = a_ref[0:1, :], a_ref[1:2, :]

    def slab(j, carry):
        cr, ci = carry
        rows = pl.ds(pl.multiple_of(j * 8, 8), 8)
        e_r, e_i = er_ref[rows, :], ei_ref[rows, :]
        for s in range(8):
            xr_ref[pl.ds(j * 8 + s, 1), :] = cr
            xi_ref[pl.ds(j * 8 + s, 1), :] = ci
            cr, ci = (ar * cr - ai * ci + e_r[s:s + 1, :], ar * ci + ai * cr + e_i[s:s + 1, :])
        return cr, ci

    lax.fori_loop(0, NC // 8, slab, (jnp.zeros((1, W), F32), jnp.zeros((1, W), F32)))
    xrb = xr_ref[...].astype(BF16)
    xib = xi_ref[...].astype(BF16)
    for a in range(L // 2):
        k = (2 * a + 2) * LANES
        y = jnp.dot(ucat[:, :k], e2_ref[(L - 2 - 2 * a) * LANES:, :], preferred_element_type=F32)
        y = y + jnp.dot(xrb, vr_ref[:, 2 * a * LANES:(2 * a + 2) * LANES], preferred_element_type=F32)
        y = y + jnp.dot(xib, vi_ref[:, 2 * a * LANES:(2 * a + 2) * LANES], preferred_element_type=F32)
        y = _gelu_tanh(y)
        ubuf[2 * a] = y[:, :LANES]
        ubuf[2 * a + 1] = y[:, LANES:]
    stores = [pltpu.make_async_copy(ubuf.at[t], y3_hbm.at[:, t, cols], sem.at[1]) for t in range(L)]
    for cp in stores:
        cp.start()
    for cp in stores:
        cp.wait()


def _ssm(u, tables):
    S, Wd = u.shape
    L, N, Q = SSM_L, SSM_N, SSM_OCT
    NC = S // L
    O = Wd // LANES
    e2rev, wr, wi, vr, vi, avec = tables
    tab = lambda a: pl.BlockSpec((None,) + a.shape[1:], lambda o: (o, 0, 0))
    y3 = pl.pallas_call(
        _ssm_body,
        grid=(O,),
        in_specs=[pl.BlockSpec(memory_space=pl.ANY)] + [tab(a) for a in (e2rev, wr, wi, vr, vi, avec)],
        out_specs=pl.BlockSpec(memory_space=pl.ANY),
        out_shape=jax.ShapeDtypeStruct((NC, L, Wd), F32),
        scratch_shapes=[pltpu.VMEM((L, NC, LANES), F32), pltpu.VMEM((NC, L * LANES), BF16)]
        + [pltpu.VMEM((NC, Q * N), F32)] * 4 + [pltpu.SemaphoreType.DMA((2,))],
        compiler_params=_cparams(1),
        name="ssm",
    )(u.reshape(NC, L, Wd), e2rev, wr, wi, vr, vi, avec)
    return y3.reshape(S, Wd)


def _mixout_body(x_ref, o1_ref, o2_ref, o3_ref, l1_ref, l2_ref, l3_ref, ys_ref,
                 wglu_ref, bglu_ref, ag_ref, sg_ref, wout_ref, mod_ref, n2g_ref, wr_ref, br_ref,
                 x1_ref, h2_ref, ri_ref, rg_ref, cnt_ref, carry_ref, *, n_heads):
    i = pl.program_id(0)
    TM = x_ref.shape[0]

    @pl.when(i == 0)
    def _():
        carry_ref[...] = jnp.zeros_like(carry_ref)

    l1, l2, l3 = l1_ref[...], l2_ref[...], l3_ref[...]
    lm = jnp.maximum(jnp.maximum(l1, l2), l3)
    e1, e2, e3 = jnp.exp(l1 - lm), jnp.exp(l2 - lm), jnp.exp(l3 - lm)
    inv = 1.0 / (e1 + e2 + e3)
    w1, w2, w3 = e1 * inv, e2 * inv, e3 * inv
    pieces = []
    for h in range(n_heads):
        hs = slice(h * HEAD_DIM, (h + 1) * HEAD_DIM)
        pieces.append(w1[:, h:h + 1] * o1_ref[:, hs].astype(F32)
                      + w2[:, h:h + 1] * o2_ref[:, hs].astype(F32)
                      + w3[:, h:h + 1] * o3_ref[:, hs].astype(F32))
    ya = _rms(jnp.concatenate(pieces, axis=1), ag_ref[...])
    ys = ys_ref[...]
    z = jnp.dot(ys.astype(BF16), wglu_ref[...], preferred_element_type=F32) + bglu_ref[...]
    yg = _rms(ys * jax.nn.sigmoid(z), sg_ref[...])
    ymix = jnp.concatenate([ya, yg], axis=1).astype(BF16)
    x1 = x_ref[...] + mod_ref[2:3, :] * jnp.dot(ymix, wout_ref[...], preferred_element_type=F32)
    x1_ref[...] = x1
    h2 = _rms(x1, n2g_ref[...]) * (1.0 + mod_ref[4:5, :]) + mod_ref[3:4, :]
    half = h2.shape[1] // 2
    h2_ref[...] = _pack_bf16_pair(h2[:, :half], h2[:, half:])
    logits = jnp.dot(h2.astype(BF16), wr_ref[...], preferred_element_type=F32) + br_ref[...]
    lane = lax.broadcasted_iota(I32, (TM, LANES), 1).astype(F32)
    work = logits
    vals, idxs = [], []
    for _ in range(TOP_K):
        m = jnp.max(work, axis=1, keepdims=True)
        idx = jnp.min(jnp.where(work == m, lane, float(LANES)), axis=1, keepdims=True)
        vals.append(m)
        idxs.append(idx)
        work = jnp.where(lane == idx, -3e38, work)
    ex = [jnp.exp(v - vals[0]) for v in vals]
    den = ex[0] + ex[1] + ex[2] + ex[3]
    onehot = jnp.zeros((TM, LANES), F32)
    for idx in idxs:
        onehot = onehot + jnp.where(lane == idx, 1.0, 0.0)
    r_i = lax.broadcasted_iota(I32, (TM, TM), 0)
    c_i = lax.broadcasted_iota(I32, (TM, TM), 1)
    tri = jnp.where(c_i < r_i, 1.0, 0.0).astype(BF16)
    before = jnp.dot(tri, onehot.astype(BF16), preferred_element_type=F32) + carry_ref[...]
    ri = jnp.zeros((TM, LANES), F32)
    rg = jnp.zeros((TM, LANES), F32)
    for k in range(TOP_K):
        rank = jnp.sum(jnp.where(lane == idxs[k], before, 0.0), axis=1, keepdims=True)
        ri = jnp.where(lane == float(k), idxs[k], ri)
        ri = jnp.where(lane == float(TOP_K + k), rank, ri)
        rg = jnp.where(lane == float(k), ex[k] / den, rg)
    ri_ref[...] = ri.astype(I32)
    rg_ref[...] = rg
    carry_ref[...] = carry_ref[...] + jnp.sum(onehot, axis=0, keepdims=True)
    cnt_ref[...] = carry_ref[...]


def _mixout(x, o, lse, ys, wglu, bglu, ag, sg, wout, mod, n2g, wr_pad, br_pad, tm):
    S, D = x.shape
    A = o[0].shape[1]
    Wd = ys.shape[1]
    row = lambda w: pl.BlockSpec((tm, w), lambda i: (i, 0))
    full = lambda a: pl.BlockSpec(a.shape, lambda i: (0,) * a.ndim)
    ins = [x, o[0], o[1], o[2], lse[0], lse[1], lse[2], ys,
           wglu, bglu, ag, sg, wout, mod, n2g, wr_pad, br_pad]
    in_specs = [row(D), row(A), row(A), row(A), row(LANES), row(LANES), row(LANES), row(Wd)]
    in_specs += [full(a) for a in ins[8:]]
    return pl.pallas_call(
        functools.partial(_mixout_body, n_heads=A // HEAD_DIM),
        grid=(S // tm,),
        in_specs=in_specs,
        out_specs=[row(D), row(D // 2), row(LANES), row(LANES), pl.BlockSpec((1, LANES), lambda i: (0, 0))],
        out_shape=[jax.ShapeDtypeStruct((S, D), F32), jax.ShapeDtypeStruct((S, D // 2), jnp.uint32),
                   jax.ShapeDtypeStruct((S, LANES), I32), jax.ShapeDtypeStruct((S, LANES), F32),
                   jax.ShapeDtypeStruct((1, LANES), F32)],
        scratch_shapes=[pltpu.VMEM((1, LANES), F32)],
        compiler_params=_cparams(1),
        name="mixout",
    )(*ins)


def _ffn_body(ie_ref, ns_ref, tok_hbm, h_hbm, wgu_ref, bgu_ref, wd_ref, bd_ref, sel_ref, o_ref,
              idx_smem, xp_ref, xb_ref, wgu_s, wd_s, sem, *, ts, sub, nf, n_items):
    i = pl.program_id(0)
    j = pl.program_id(1)
    nsub = ns_ref[i]
    slot = lax.rem(i, 2)
    nxt = jnp.minimum(i + 1, n_items - 1)
    nsub_nxt = jnp.where(i + 1 < n_items, ns_ref[nxt], 0)

    def fetch_rows(item, item_nsub, sl):
        base = pl.multiple_of(sl * ts, ts)
        cp = pltpu.make_async_copy(tok_hbm.at[pl.ds(pl.multiple_of(item * ts, ts), ts)],
                                   idx_smem.at[pl.ds(base, ts)], sem.at[2])
        cp.start()
        cp.wait()

        def issue8(r8, c):
            for k in range(8):
                r = r8 * 8 + k
                pltpu.make_async_copy(h_hbm.at[idx_smem[base + r]], xp_ref.at[sl, r], sem.at[sl]).start()
            return c

        lax.fori_loop(0, item_nsub * (sub // 8), issue8, 0)

    @pl.when(j == 0)
    def _():
        o_ref[...] = jnp.broadcast_to(bd_ref[...], o_ref.shape)

    @pl.when((i == 0) & (j == 0))
    def _():
        fetch_rows(0, nsub, 0)

    @pl.when((nsub > 0) & (j == 0))
    def _():
        def drain(r, c):
            pltpu.make_async_copy(h_hbm.at[0], xp_ref.at[slot, 0], sem.at[slot]).wait()
            return c

        lax.fori_loop(0, nsub * sub, drain, 0)

        def unpack(s, c):
            rows = pl.ds(pl.multiple_of(s * sub, sub), sub)
            lo, hi = _unpack_bf16_pair(xp_ref[slot, rows, :])
            xb_ref[rows, :] = jnp.concatenate([lo, hi], axis=1).astype(BF16)
            return c

        lax.fori_loop(0, nsub, unpack, 0)

    @pl.when((j == 1) & (nsub_nxt > 0))
    def _():
        fetch_rows(nxt, nsub_nxt, 1 - slot)

    @pl.when(nsub > 0)
    def _():
        wgu = wgu_ref[...].astype(BF16)
        wd = wd_ref[...].astype(BF16)
        wgu_s[...] = wgu
        wd_s[...] = wd
        bgu = bgu_ref[...]
        n2 = wgu.shape[1]

        def gate_up(s, w):
            rows = pl.ds(pl.multiple_of(s * sub, sub), sub)
            gu = jnp.dot(xb_ref[rows, :], w, preferred_element_type=F32) + bgu
            glu = jnp.minimum(gu, SWIGLU_LIMIT)
            a = glu * jax.nn.sigmoid(SWIGLU_ALPHA * glu)
            b = jnp.clip(gu, -SWIGLU_LIMIT, SWIGLU_LIMIT) + 1.0
            prod = (a * pltpu.roll(b, n2 - 1, axis=1)).astype(BF16)
            return jnp.dot(prod, sel_ref[...], preferred_element_type=F32).astype(BF16)

        def down(s, act, w):
            rows = pl.ds(pl.multiple_of(s * sub, sub), sub)
            o_ref[rows, :] = o_ref[rows, :] + jnp.dot(act, w, preferred_element_type=F32)

        def block(s, act):
            down(s - 1, act, wd_s[...])
            return gate_up(s, wgu_s[...])

        act = lax.fori_loop(1, nsub, block, gate_up(0, wgu))
        down(nsub - 1, act, wd_s[...])


def _ffn(item_e, item_nsub, row_tok, h2p, wgu, bgu, wd, bd, ts, sub, fc):
    E, D, F2 = wgu.shape
    F = F2 // 2
    nf = F // fc
    n_items = item_e.shape[0]
    sel = (jnp.arange(2 * fc)[:, None] == 2 * jnp.arange(fc)[None, :]).astype(BF16)

    def chunk(i, j, ie, ns):
        return jnp.where(ns[i] > 0, j, nf - 1)

    grid_spec = pltpu.PrefetchScalarGridSpec(
        num_scalar_prefetch=2,
        grid=(n_items, nf),
        in_specs=[pl.BlockSpec(memory_space=pl.ANY),
                  pl.BlockSpec(memory_space=pl.ANY),
                  pl.BlockSpec((None, D, 2 * fc), lambda i, j, ie, ns: (ie[i], 0, chunk(i, j, ie, ns))),
                  pl.BlockSpec((None, 1, 2 * fc), lambda i, j, ie, ns: (ie[i], 0, chunk(i, j, ie, ns))),
                  pl.BlockSpec((None, fc, D), lambda i, j, ie, ns: (ie[i], chunk(i, j, ie, ns), 0)),
                  pl.BlockSpec((None, 1, D), lambda i, j, ie, ns: (ie[i], 0, 0)),
                  pl.BlockSpec((2 * fc, fc), lambda i, j, ie, ns: (0, 0))],
        out_specs=pl.BlockSpec((ts, D), lambda i, j, ie, ns: (i, 0)),
        scratch_shapes=[pltpu.SMEM((2 * ts,), I32),
                        pltpu.VMEM((2, ts, D // 2), jnp.uint32),
                        pltpu.VMEM((ts, D), BF16),
                        pltpu.VMEM((D, 2 * fc), BF16),
                        pltpu.VMEM((fc, D), BF16),
                        pltpu.SemaphoreType.DMA((3,))],
    )
    return pl.pallas_call(
        functools.partial(_ffn_body, ts=ts, sub=sub, nf=nf, n_items=n_items),
        grid_spec=grid_spec,
        out_shape=jax.ShapeDtypeStruct((n_items * ts, D), F32),
        compiler_params=_cparams(2),
        name="ffn",
    )(item_e, item_nsub, row_tok, h2p, wgu, bgu.reshape(E, 1, F2), wd, bd.reshape(E, 1, D), sel)


def _combine_body(dest_hbm, ys_hbm, x1_ref, rg_ref, mod_ref, fg_ref, o_ref, idx_smem, buf_ref, sem, *, tm):
    i = pl.program_id(0)
    n = TOP_K * tm
    cp = pltpu.make_async_copy(dest_hbm.at[pl.ds(pl.multiple_of(i * n, n), n)], idx_smem, sem.at[0])
    cp.start()
    cp.wait()

    def issue(r, c):
        pltpu.make_async_copy(ys_hbm.at[idx_smem[r]], buf_ref.at[r], sem.at[1]).start()
        return c

    lax.fori_loop(0, n, issue, 0)

    def drain(r, c):
        pltpu.make_async_copy(ys_hbm.at[0], buf_ref.at[r], sem.at[1]).wait()
        return c

    lax.fori_loop(0, n, drain, 0)
    rg = rg_ref[...]
    y = jnp.zeros(x1_ref.shape, F32)
    for k in range(TOP_K):
        y = y + rg[:, k:k + 1] * buf_ref[k * tm:(k + 1) * tm, :]
    x2 = x1_ref[...] + mod_ref[5:6, :] * y
    o_ref[...] = _rms(x2, fg_ref[...])


def _combine(dest_km, y_sorted, x1, rg, mod, fg, tm):
    S, D = x1.shape
    return pl.pallas_call(
        functools.partial(_combine_body, tm=tm),
        grid=(S // tm,),
        in_specs=[pl.BlockSpec(memory_space=pl.ANY),
                  pl.BlockSpec(memory_space=pl.ANY),
                  pl.BlockSpec((tm, D), lambda i: (i, 0)),
                  pl.BlockSpec((tm, LANES), lambda i: (i, 0)),
                  pl.BlockSpec((6, D), lambda i: (0, 0)),
                  pl.BlockSpec((1, D), lambda i: (0, 0))],
        out_specs=pl.BlockSpec((tm, D), lambda i: (i, 0)),
        out_shape=jax.ShapeDtypeStruct((S, D), F32),
        scratch_shapes=[pltpu.SMEM((TOP_K * tm,), I32),
                        pltpu.VMEM((TOP_K * tm, D), F32),
                        pltpu.SemaphoreType.DMA((2,))],
        compiler_params=_cparams(1),
        name="combine",
    )(dest_km, y_sorted, x1, rg, mod, fg.reshape(1, D))


def _routing_tables(ri, counts_f, n_experts, ts, sub, tm_c):
    S = ri.shape[0]
    counts = counts_f[0, :n_experts].astype(I32)
    nblk = (counts + ts - 1) // ts
    blk_end = jnp.cumsum(nblk)
    blk_start = blk_end - nblk
    e_idx = ri[:, :TOP_K]
    rank = ri[:, TOP_K:2 * TOP_K]
    dest = blk_start[e_idx] * ts + rank
    max_items = (S * TOP_K) // ts + n_experts
    tok = jnp.broadcast_to(jnp.arange(S, dtype=I32)[:, None], (S, TOP_K))
    row_tok = jnp.zeros((max_items * ts,), I32).at[dest.reshape(-1)].set(tok.reshape(-1))
    item = jnp.arange(max_items, dtype=I32)
    n_items = blk_end[-1]
    item_c = jnp.minimum(item, n_items - 1)
    item_e = jnp.minimum(jnp.searchsorted(blk_end, item_c, side='right'), n_experts - 1).astype(I32)
    rows_left = counts[item_e] - (item_c - blk_start[item_e]) * ts
    nsub = jnp.clip((rows_left + sub - 1) // sub, 0, ts // sub)
    item_nsub = jnp.where(item < n_items, nsub, 0).astype(I32)
    dest_km = dest.reshape(S // tm_c, tm_c, TOP_K).transpose(0, 2, 1).reshape(-1)
    return item_e, item_nsub, row_tok, dest_km


def _forward(x, c, w_ada, b_ada, norm1_g, w_in, lambda_re, lambda_im, ssm_b_re, ssm_b_im,
             ssm_c_re, ssm_c_im, ssm_d, ssm_log_dt, w_glu, b_glu, attn_out_g, ssm_out_g,
             w_out, norm2_g, w_router, b_router, w_gate_up, b_gate_up, w_down, b_down, final_g,
             *, tiles):
    B, S, D = x.shape
    assert B == 1 and w_ada.shape[0] == 1
    A = attn_out_g.shape[1]
    Wd = ssm_out_g.shape[1]
    assert A == Wd and w_in.shape[2] == 3 * A + Wd
    E = w_router.shape[2]
    xs = x.reshape(S, D)

    mod = _adaln(c, w_ada[0], b_ada[0]).reshape(6, D)
    w_in_b = w_in[0].astype(BF16)
    qkv = _inproj(xs, norm1_g[0], mod, w_in_b[:, :3 * A], tiles['tm_in'], A, BF16, "inproj_qkv")
    u = _inproj(xs, norm1_g[0], mod, w_in_b[:, 3 * A:], tiles['tm_in'], Wd, F32, "inproj_u")
    branches = [_attn_branch(qkv, d, A) for d in ATTN_DILATIONS]
    o = [b[0] for b in branches]
    lse = [b[1] for b in branches]

    tables = _ssm_tables(lambda_re[0], lambda_im[0], ssm_b_re[0], ssm_b_im[0], ssm_c_re[0],
                         ssm_c_im[0], ssm_d[0], ssm_log_dt[0])
    ys = _ssm(u, tables)

    wr_pad = jnp.zeros((D, LANES), BF16).at[:, :E].set(w_router[0].astype(BF16))
    br_pad = jnp.full((1, LANES), NEG_BIG, F32).at[0, :E].set(b_router[0])
    x1, h2, ri, rg, counts = _mixout(
        xs, o, lse, ys, w_glu[0].astype(BF16), b_glu[0].reshape(1, Wd), attn_out_g[0].reshape(1, A),
        ssm_out_g[0].reshape(1, Wd), w_out[0].astype(BF16), mod, norm2_g[0].reshape(1, D),
        wr_pad, br_pad, tiles['tm_mix'])

    ts, sub, fc, tm_c = tiles['ts'], tiles['sub'], tiles['fc'], tiles['tm_c']
    item_e, item_nsub, row_tok, dest_km = _routing_tables(ri, counts, E, ts, sub, tm_c)
    y_sorted = _ffn(item_e, item_nsub, row_tok, h2, w_gate_up[0], b_gate_up[0], w_down[0], b_down[0],
                    ts, sub, fc)
    out = _combine(dest_km, y_sorted, x1, rg, mod, final_g, tm_c)
    return out.reshape(B, S, D)


TILES = dict(tm_in=512, tm_mix=256, ts=1024, sub=256, fc=256, tm_c=256)


def kernel(x, c, w_ada, b_ada, norm1_g, w_in, lambda_re, lambda_im, ssm_b_re, ssm_b_im, ssm_c_re, ssm_c_im, ssm_d, ssm_log_dt, w_glu, b_glu, attn_out_g, ssm_out_g, w_out, norm2_g, w_router, b_router, w_gate_up, b_gate_up, w_down, b_down, final_g):
    return _forward(x, c, w_ada, b_ada, norm1_g, w_in, lambda_re, lambda_im, ssm_b_re, ssm_b_im,
                    ssm_c_re, ssm_c_im, ssm_d, ssm_log_dt, w_glu, b_glu, attn_out_g, ssm_out_g,
                    w_out, norm2_g, w_router, b_router, w_gate_up, b_gate_up, w_down, b_down, final_g,
                    tiles=TILES)
```

```python
import functools
import math

import jax
import jax.numpy as jnp
from jax import lax
from jax.experimental import pallas as pl
from jax.experimental.pallas import tpu as pltpu

F32 = jnp.float32
BF16 = jnp.bfloat16
I32 = jnp.int32

HEAD_DIM = 64
ATTN_BLOCK = 128
ATTN_DILATIONS = (1, 4, 16)
SSM_P = 16
SSM_N = 64
SSM_L = 16
TOP_K = 4
SWIGLU_LIMIT = 7.0
SWIGLU_ALPHA = 1.702
NORM_EPS = 1e-6
LANES = 128
SSM_OCT = LANES // SSM_P
NEG_BIG = -1e30
VMEM_LIMIT = 56 * 1024 * 1024
FFN_VMEM_LIMIT = 60 * 1024 * 1024
SMEM_I32_TILE = 1024


def _cparams(n_axes, vmem=VMEM_LIMIT):
    return pltpu.CompilerParams(dimension_semantics=("arbitrary",) * n_axes,
                                vmem_limit_bytes=vmem)


def _rms(x, g):
    return x * lax.rsqrt(jnp.mean(x * x, axis=-1, keepdims=True) + NORM_EPS) * g


def _pack_bf16_pair(lo, hi):
    lb = lax.bitcast_convert_type(lo.astype(BF16).astype(F32), jnp.uint32)
    hb = lax.bitcast_convert_type(hi.astype(BF16).astype(F32), jnp.uint32)
    return (lb >> 16) | (hb & jnp.uint32(0xFFFF0000))


def _unpack_bf16_pair(w):
    return (lax.bitcast_convert_type(w << 16, F32),
            lax.bitcast_convert_type(w & jnp.uint32(0xFFFF0000), F32))


def _adaln_body(c_ref, w_ref, b_ref, o_ref):
    c = c_ref[...]
    ca = c * jax.nn.sigmoid(c)
    o_ref[...] = jnp.sum(ca * w_ref[...], axis=0, keepdims=True) + b_ref[...]


def _adaln(c, w, b):
    D, N = w.shape
    bn = next(b for b in (1024, 512, 256, LANES) if N % b == 0)
    return pl.pallas_call(
        _adaln_body,
        grid=(N // bn,),
        in_specs=[pl.BlockSpec((D, 1), lambda j: (0, 0)),
                  pl.BlockSpec((D, bn), lambda j: (0, j)),
                  pl.BlockSpec((1, bn), lambda j: (0, j))],
        out_specs=pl.BlockSpec((1, bn), lambda j: (0, j)),
        out_shape=jax.ShapeDtypeStruct((1, N), F32),
        compiler_params=_cparams(1),
        name="adaln",
    )(c.reshape(D, 1), w, b.reshape(1, N))


def _inproj_body(x_ref, g_ref, mod_ref, w_ref, o_ref):
    h = _rms(x_ref[...], g_ref[...]) * (1.0 + mod_ref[1:2, :]) + mod_ref[0:1, :]
    o_ref[...] = jnp.dot(h.astype(BF16), w_ref[...], preferred_element_type=F32).astype(o_ref.dtype)


def _inproj(x, g, mod, w_bf16, tm, tn, out_dtype, name):
    S, D = x.shape
    N = w_bf16.shape[1]
    return pl.pallas_call(
        _inproj_body,
        grid=(N // tn, S // tm),
        in_specs=[pl.BlockSpec((tm, D), lambda j, i: (i, 0)),
                  pl.BlockSpec((1, D), lambda j, i: (0, 0)),
                  pl.BlockSpec((6, D), lambda j, i: (0, 0)),
                  pl.BlockSpec((D, tn), lambda j, i: (0, j))],
        out_specs=pl.BlockSpec((tm, tn), lambda j, i: (i, j)),
        out_shape=jax.ShapeDtypeStruct((S, N), out_dtype),
        compiler_params=_cparams(2),
        name=name,
    )(x, g.reshape(1, D), mod, w_bf16)


def _attn_body(q_ref, kp_ref, kc_ref, vp_ref, vc_ref, o_ref, lse_ref, *, n_heads):
    n = pl.program_id(1)
    B = ATTN_BLOCK
    row = lax.broadcasted_iota(I32, (B, 2 * B), 0)
    col = lax.broadcasted_iota(I32, (B, 2 * B), 1)
    mask = (col >= row) & (col <= row + B) & ((col >= B) | (n > 0))
    lane = lax.broadcasted_iota(I32, (B, LANES), 1)
    lse_tile = jnp.zeros((B, LANES), F32)
    scale = HEAD_DIM ** -0.5
    for h in range(n_heads):
        hs = slice(h * HEAD_DIM, (h + 1) * HEAD_DIM)
        q = q_ref[:, hs]
        k = jnp.concatenate([kp_ref[:, hs], kc_ref[:, hs]], axis=0)
        v = jnp.concatenate([vp_ref[:, hs], vc_ref[:, hs]], axis=0)
        s = lax.dot_general(q, k, (((1,), (1,)), ((), ())), preferred_element_type=F32) * scale
        s = jnp.where(mask, s, NEG_BIG)
        m = jnp.max(s, axis=-1, keepdims=True)
        p = jnp.exp(s - m)
        l = jnp.sum(p, axis=-1, keepdims=True)
        o = jnp.dot(p.astype(BF16), v, preferred_element_type=F32) / l
        o_ref[:, hs] = o.astype(BF16)
        lse_tile = jnp.where(lane == h, m + jnp.log(l), lse_tile)
    lse_ref[...] = lse_tile


def _attn_branch(proj, dil, A):
    S = proj.shape[0]
    rows = S // dil
    nb = rows // ATTN_BLOCK
    pv = proj.reshape(rows, dil * 3 * A)
    blk = (ATTN_BLOCK, A)
    cur = lambda c: (lambda r, n: (n, r * 3 + c))
    prev = lambda c: (lambda r, n: (jnp.maximum(n - 1, 0), r * 3 + c))
    o, lse = pl.pallas_call(
        functools.partial(_attn_body, n_heads=A // HEAD_DIM),
        grid=(dil, nb),
        in_specs=[pl.BlockSpec(blk, cur(0)),
                  pl.BlockSpec(blk, prev(1)), pl.BlockSpec(blk, cur(1)),
                  pl.BlockSpec(blk, prev(2)), pl.BlockSpec(blk, cur(2))],
        out_specs=[pl.BlockSpec(blk, lambda r, n: (n, r)),
                   pl.BlockSpec((ATTN_BLOCK, LANES), lambda r, n: (n, r))],
        out_shape=[jax.ShapeDtypeStruct((rows, dil * A), BF16),
                   jax.ShapeDtypeStruct((rows, dil * LANES), F32)],
        compiler_params=_cparams(2),
        name=f"attn_d{dil}",
    )(pv, pv, pv, pv, pv)
    return o.reshape(S, A), lse.reshape(S, LANES)


def _ssm_tables(lam_re, lam_im, b_re, b_im, c_re, c_im, d_skip, log_dt):
    hp = lax.Precision.HIGHEST
    G, N = lam_re.shape
    P, L, Q = SSM_P, SSM_L, SSM_OCT
    O = G // Q
    dt = jnp.exp(log_dt)[:, None]
    lr, li = lam_re, lam_im

    def powers(j):
        j = j.astype(F32)[:, None, None]
        mag = jnp.exp(j * (lr * dt))
        return mag * jnp.cos(j * (li * dt)), mag * jnp.sin(j * (li * dt))

    def c_times(pr, pi):
        return (c_re[None] * pr[:, :, None, :] - c_im[None] * pi[:, :, None, :],
                c_re[None] * pi[:, :, None, :] + c_im[None] * pr[:, :, None, :])

    one_r, one_i = powers(jnp.ones((1,), I32))
    den = lr * lr + li * li
    nr, ni = one_r[0] - 1.0, one_i[0]
    f_re = (nr * lr + ni * li) / den
    f_im = (ni * lr - nr * li) / den
    bb_re = f_re[..., None] * b_re - f_im[..., None] * b_im
    bb_im = f_re[..., None] * b_im + f_im[..., None] * b_re
    down = (L - 1) - jnp.arange(L, dtype=I32)
    dr, di = powers(down)
    cr_, ci_ = c_times(dr, di)
    kk = (jnp.einsum('jgpn,gnq->jgpq', cr_, bb_re, precision=hp)
          - jnp.einsum('jgpn,gnq->jgpq', ci_, bb_im, precision=hp))
    skip = d_skip[:, :, None] * jnp.eye(P, dtype=F32)[None]
    kk = kk + jnp.where((down == 0)[:, None, None, None], skip[None], 0.0)
    eye = jnp.eye(Q, dtype=F32)
    dd = jnp.einsum('ab,joapq->ojaqbp', eye, kk.reshape(L, O, Q, P, P)).reshape(O, L, LANES, LANES)
    left = jnp.concatenate([dd[:, 1:], jnp.zeros_like(dd[:, :1])], axis=1)
    e2rev = jnp.concatenate([left, dd], axis=-1).reshape(O, L * LANES, 2 * LANES)
    prr, pir = dr[:, :, :, None], di[:, :, :, None]
    w_r = (prr * bb_re[None] - pir * bb_im[None]).reshape(L, O, Q, N, P)
    w_i = (prr * bb_im[None] + pir * bb_re[None]).reshape(L, O, Q, N, P)
    wr = jnp.einsum('ab,soanq->osaqbn', eye, w_r).reshape(O, L * LANES, Q * N)
    wi = jnp.einsum('ab,soanq->osaqbn', eye, w_i).reshape(O, L * LANES, Q * N)
    ur, ui = powers(1 + jnp.arange(L, dtype=I32))
    vr_, vi_ = c_times(ur, ui)
    vr = jnp.einsum('ab,toapn->oantbp', eye, vr_.reshape(L, O, Q, P, N)).reshape(O, Q * N, L * LANES)
    vi = -jnp.einsum('ab,toapn->oantbp', eye, vi_.reshape(L, O, Q, P, N)).reshape(O, Q * N, L * LANES)
    pr_l, pi_l = ur[L - 1], ui[L - 1]
    avec = jnp.stack([pr_l.reshape(O, Q * N), pi_l.reshape(O, Q * N)], axis=1)
    return (e2rev.astype(BF16), wr.astype(BF16), wi.astype(BF16), vr.astype(BF16), vi.astype(BF16), avec)


def _gelu_tanh(x):
    return 0.5 * x * (1.0 + jnp.tanh(math.sqrt(2.0 / math.pi) * (x + 0.044715 * (x * x * x))))


def _ssm_body(u3_hbm, e2_ref, wr_ref, wi_ref, vr_ref, vi_ref, a_ref, y3_hbm,
              ubuf, ucat, er_ref, ei_ref, xr_ref, xi_ref, sem):
    o = pl.program_id(0)
    L = SSM_L
    NC, W = er_ref.shape
    cols = pl.ds(pl.multiple_of(o * LANES, LANES), LANES)
    loads = [pltpu.make_async_copy(u3_hbm.at[:, t, cols], ubuf.at[t], sem.at[0]) for t in range(L)]
    for cp in loads:
        cp.start()
    for cp in loads:
        cp.wait()
    for t in range(L):
        ucat[:, t * LANES:(t + 1) * LANES] = ubuf[t].astype(BF16)
    er_ref[...] = jnp.dot(ucat[...], wr_ref[...], preferred_element_type=F32)
    ei_ref[...] = jnp.dot(ucat[...], wi_ref[...], preferred_element_type=F32)
    ar, ai = a_ref[0:1, :], a_ref[1:2, :]

    def slab(j, carry):
        cr, ci = carry
        rows = pl.ds(pl.multiple_of(j * 8, 8), 8)
        e_r, e_i = er_ref[rows, :], ei_ref[rows, :]
        for s in range(8):
            xr_ref[pl.ds(j * 8 + s, 1), :] = cr
            xi_ref[pl.ds(j * 8 + s, 1), :] = ci
            cr, ci = (ar * cr - ai * ci + e_r[s:s + 1, :], ar * ci + ai * cr + e_i[s:s + 1, :])
        return cr, ci

    lax.fori_loop(0, NC // 8, slab, (jnp.zeros((1, W), F32), jnp.zeros((1, W), F32)))
    xrb = xr_ref[...].astype(BF16)
    xib = xi_ref[...].astype(BF16)
    for a in range(L // 2):
        k = (2 * a + 2) * LANES
        y = jnp.dot(ucat[:, :k], e2_ref[(L - 2 - 2 * a) * LANES:, :], preferred_element_type=F32)
        y = y + jnp.dot(xrb, vr_ref[:, 2 * a * LANES:(2 * a + 2) * LANES], preferred_element_type=F32)
        y = y + jnp.dot(xib, vi_ref[:, 2 * a * LANES:(2 * a + 2) * LANES], preferred_element_type=F32)
        y = _gelu_tanh(y)
        ubuf[2 * a] = y[:, :LANES]
        ubuf[2 * a + 1] = y[:, LANES:]
    stores = [pltpu.make_async_copy(ubuf.at[t], y3_hbm.at[:, t, cols], sem.at[1]) for t in range(L)]
    for cp in stores:
        cp.start()
    for cp in stores:
        cp.wait()


def _ssm(u, tables):
    S, Wd = u.shape
    L, N, Q = SSM_L, SSM_N, SSM_OCT
    NC = S // L
    O = Wd // LANES
    e2rev, wr, wi, vr, vi, avec = tables
    tab = lambda a: pl.BlockSpec((None,) + a.shape[1:], lambda o: (o, 0, 0))
    y3 = pl.pallas_call(
        _ssm_body,
        grid=(O,),
        in_specs=[pl.BlockSpec(memory_space=pl.ANY)] + [tab(a) for a in (e2rev, wr, wi, vr, vi, avec)],
        out_specs=pl.BlockSpec(memory_space=pl.ANY),
        out_shape=jax.ShapeDtypeStruct((NC, L, Wd), F32),
        scratch_shapes=[pltpu.VMEM((L, NC, LANES), F32), pltpu.VMEM((NC, L * LANES), BF16)]
        + [pltpu.VMEM((NC, Q * N), F32)] * 4 + [pltpu.SemaphoreType.DMA((2,))],
        compiler_params=_cparams(1),
        name="ssm",
    )(u.reshape(NC, L, Wd), e2rev, wr, wi, vr, vi, avec)
    return y3.reshape(S, Wd)


def _mixout_body(x_ref, o1_ref, o2_ref, o3_ref, l1_ref, l2_ref, l3_ref, ys_ref,
                 wglu_ref, bglu_ref, ag_ref, sg_ref, wout_ref, mod_ref, n2g_ref, wr_ref, br_ref,
                 x1_ref, h2_ref, ri_ref, rg_ref, cnt_ref, carry_ref, *, n_heads):
    i = pl.program_id(0)
    TM = x_ref.shape[0]

    @pl.when(i == 0)
    def _():
        carry_ref[...] = jnp.zeros_like(carry_ref)

    l1, l2, l3 = l1_ref[...], l2_ref[...], l3_ref[...]
    lm = jnp.maximum(jnp.maximum(l1, l2), l3)
    e1, e2, e3 = jnp.exp(l1 - lm), jnp.exp(l2 - lm), jnp.exp(l3 - lm)
    inv = 1.0 / (e1 + e2 + e3)
    w1, w2, w3 = e1 * inv, e2 * inv, e3 * inv
    pieces = []
    for h in range(n_heads):
        hs = slice(h * HEAD_DIM, (h + 1) * HEAD_DIM)
        pieces.append(w1[:, h:h + 1] * o1_ref[:, hs].astype(F32)
                      + w2[:, h:h + 1] * o2_ref[:, hs].astype(F32)
                      + w3[:, h:h + 1] * o3_ref[:, hs].astype(F32))
    ya = _rms(jnp.concatenate(pieces, axis=1), ag_ref[...])
    ys = ys_ref[...]
    z = jnp.dot(ys.astype(BF16), wglu_ref[...], preferred_element_type=F32) + bglu_ref[...]
    yg = _rms(ys * jax.nn.sigmoid(z), sg_ref[...])
    ymix = jnp.concatenate([ya, yg], axis=1).astype(BF16)
    x1 = x_ref[...] + mod_ref[2:3, :] * jnp.dot(ymix, wout_ref[...], preferred_element_type=F32)
    x1_ref[...] = x1
    h2 = _rms(x1, n2g_ref[...]) * (1.0 + mod_ref[4:5, :]) + mod_ref[3:4, :]
    half = h2.shape[1] // 2
    h2_ref[...] = _pack_bf16_pair(h2[:, :half], h2[:, half:])
    logits = jnp.dot(h2.astype(BF16), wr_ref[...], preferred_element_type=F32) + br_ref[...]
    lane = lax.broadcasted_iota(I32, (TM, LANES), 1).astype(F32)
    work = logits
    vals, idxs = [], []
    for _ in range(TOP_K):
        m = jnp.max(work, axis=1, keepdims=True)
        idx = jnp.min(jnp.where(work == m, lane, float(LANES)), axis=1, keepdims=True)
        vals.append(m)
        idxs.append(idx)
        work = jnp.where(lane == idx, -3e38, work)
    ex = [jnp.exp(v - vals[0]) for v in vals]
    den = ex[0] + ex[1] + ex[2] + ex[3]
    onehot = jnp.zeros((TM, LANES), F32)
    for idx in idxs:
        onehot = onehot + jnp.where(lane == idx, 1.0, 0.0)
    r_i = lax.broadcasted_iota(I32, (TM, TM), 0)
    c_i = lax.broadcasted_iota(I32, (TM, TM), 1)
    tri = jnp.where(c_i < r_i, 1.0, 0.0).astype(BF16)
    before = jnp.dot(tri, onehot.astype(BF16), preferred_element_type=F32) + carry_ref[...]
    ri = jnp.zeros((TM, LANES), F32)
    rg = jnp.zeros((TM, LANES), F32)
    for k in range(TOP_K):
        rank = jnp.sum(jnp.where(lane == idxs[k], before, 0.0), axis=1, keepdims=True)
        ri = jnp.where(lane == float(k), idxs[k], ri)
        ri = jnp.where(lane == float(TOP_K + k), rank, ri)
        rg = jnp.where(lane == float(k), ex[k] / den, rg)
    ri_ref[...] = ri.astype(I32)
    rg_ref[...] = rg
    carry_ref[...] = carry_ref[...] + jnp.sum(onehot, axis=0, keepdims=True)
    cnt_ref[...] = carry_ref[...]


def _mixout(x, o, lse, ys, wglu, bglu, ag, sg, wout, mod, n2g, wr_pad, br_pad, tm):
    S, D = x.shape
    A = o[0].shape[1]
    Wd = ys.shape[1]
    row = lambda w: pl.BlockSpec((tm, w), lambda i: (i, 0))
    full = lambda a: pl.BlockSpec(a.shape, lambda i: (0,) * a.ndim)
    ins = [x, o[0], o[1], o[2], lse[0], lse[1], lse[2], ys,
           wglu, bglu, ag, sg, wout, mod, n2g, wr_pad, br_pad]
    in_specs = [row(D), row(A), row(A), row(A), row(LANES), row(LANES), row(LANES), row(Wd)]
    in_specs += [full(a) for a in ins[8:]]
    return pl.pallas_call(
        functools.partial(_mixout_body, n_heads=A // HEAD_DIM),
        grid=(S // tm,),
        in_specs=in_specs,
        out_specs=[row(D), row(D // 2), row(LANES), row(LANES), pl.BlockSpec((1, LANES), lambda i: (0, 0))],
        out_shape=[jax.ShapeDtypeStruct((S, D), F32), jax.ShapeDtypeStruct((S, D // 2), jnp.uint32),
                   jax.ShapeDtypeStruct((S, LANES), I32), jax.ShapeDtypeStruct((S, LANES), F32),
                   jax.ShapeDtypeStruct((1, LANES), F32)],
        scratch_shapes=[pltpu.VMEM((1, LANES), F32)],
        compiler_params=_cparams(1),
        name="mixout",
    )(*ins)


def _ffn_body(ie_ref, ns_ref, tok_hbm, h_hbm, wgu_ref, bgu_ref, wd_ref, bd_ref, sel_ref, o_ref,
              idx_smem, xp_ref, xb_ref, wgu_s, wd_s, sem, *, ts, tsp, sub, nf, n_items):
    i = pl.program_id(0)
    j = pl.program_id(1)
    nsub = ns_ref[i]
    slot = lax.rem(i, 2)
    nxt = jnp.minimum(i + 1, n_items - 1)
    nsub_nxt = jnp.where(i + 1 < n_items, ns_ref[nxt], 0)

    def fetch_rows(item, item_nsub, sl):
        base = pl.multiple_of(sl * tsp, tsp)
        cp = pltpu.make_async_copy(tok_hbm.at[pl.ds(pl.multiple_of(item * tsp, tsp), tsp)],
                                   idx_smem.at[pl.ds(base, tsp)], sem.at[2])
        cp.start()
        cp.wait()

        def issue8(r8, c):
            for k in range(8):
                r = r8 * 8 + k
                pltpu.make_async_copy(h_hbm.at[idx_smem[base + r]], xp_ref.at[sl, r], sem.at[sl]).start()
            return c

        lax.fori_loop(0, item_nsub * (sub // 8), issue8, 0)

    @pl.when(j == 0)
    def _():
        o_ref[...] = jnp.broadcast_to(bd_ref[...], o_ref.shape)

    @pl.when((i == 0) & (j == 0))
    def _():
        fetch_rows(0, nsub, 0)

    @pl.when((nsub > 0) & (j == 0))
    def _():
        def drain(s, c):
            pltpu.make_async_copy(h_hbm.at[pl.ds(0, sub)], xp_ref.at[slot, pl.ds(0, sub)], sem.at[slot]).wait()
            return c

        lax.fori_loop(0, nsub, drain, 0)

        def unpack(s, c):
            rows = pl.ds(pl.multiple_of(s * sub, sub), sub)
            lo, hi = _unpack_bf16_pair(xp_ref[slot, rows, :])
            xb_ref[rows, :] = jnp.concatenate([lo, hi], axis=1).astype(BF16)
            return c

        lax.fori_loop(0, nsub, unpack, 0)

    @pl.when((j == 1) & (nsub_nxt > 0))
    def _():
        fetch_rows(nxt, nsub_nxt, 1 - slot)

    @pl.when(nsub > 0)
    def _():
        wgu = wgu_ref[...].astype(BF16)
        wd = wd_ref[...].astype(BF16)
        wgu_s[...] = wgu
        wd_s[...] = wd
        bgu = bgu_ref[...]
        n2 = wgu.shape[1]

        def gate_up(s, w):
            rows = pl.ds(pl.multiple_of(s * sub, sub), sub)
            gu = jnp.dot(xb_ref[rows, :], w, preferred_element_type=F32) + bgu
            glu = jnp.minimum(gu, SWIGLU_LIMIT)
            a = glu * jax.nn.sigmoid(SWIGLU_ALPHA * glu)
            b = jnp.clip(gu, -SWIGLU_LIMIT, SWIGLU_LIMIT) + 1.0
            prod = (a * pltpu.roll(b, n2 - 1, axis=1)).astype(BF16)
            return jnp.dot(prod, sel_ref[...], preferred_element_type=F32).astype(BF16)

        def down(s, act, w):
            rows = pl.ds(pl.multiple_of(s * sub, sub), sub)
            o_ref[rows, :] = o_ref[rows, :] + jnp.dot(act, w, preferred_element_type=F32)

        def block(s, act):
            down(s - 1, act, wd_s[...])
            return gate_up(s, wgu_s[...])

        act = lax.fori_loop(1, nsub, block, gate_up(0, wgu))
        down(nsub - 1, act, wd_s[...])


def _ffn(item_e, item_nsub, row_tok, h2p, wgu, bgu, wd, bd, ts, tsp, sub, fc):
    E, D, F2 = wgu.shape
    F = F2 // 2
    nf = F // fc
    n_items = item_e.shape[0]
    sel = (jnp.arange(2 * fc)[:, None] == 2 * jnp.arange(fc)[None, :]).astype(BF16)

    def chunk(i, j, ie, ns):
        return jnp.where(ns[i] > 0, j, nf - 1)

    grid_spec = pltpu.PrefetchScalarGridSpec(
        num_scalar_prefetch=2,
        grid=(n_items, nf),
        in_specs=[pl.BlockSpec(memory_space=pl.ANY),
                  pl.BlockSpec(memory_space=pl.ANY),
                  pl.BlockSpec((None, D, 2 * fc), lambda i, j, ie, ns: (ie[i], 0, chunk(i, j, ie, ns))),
                  pl.BlockSpec((None, 1, 2 * fc), lambda i, j, ie, ns: (ie[i], 0, chunk(i, j, ie, ns))),
                  pl.BlockSpec((None, fc, D), lambda i, j, ie, ns: (ie[i], chunk(i, j, ie, ns), 0)),
                  pl.BlockSpec((None, 1, D), lambda i, j, ie, ns: (ie[i], 0, 0)),
                  pl.BlockSpec((2 * fc, fc), lambda i, j, ie, ns: (0, 0))],
        out_specs=pl.BlockSpec((ts, D), lambda i, j, ie, ns: (i, 0)),
        scratch_shapes=[pltpu.SMEM((2 * tsp,), I32),
                        pltpu.VMEM((2, ts, D // 2), jnp.uint32),
                        pltpu.VMEM((ts, D), BF16),
                        pltpu.VMEM((D, 2 * fc), BF16),
                        pltpu.VMEM((fc, D), BF16),
                        pltpu.SemaphoreType.DMA((3,))],
    )
    return pl.pallas_call(
        functools.partial(_ffn_body, ts=ts, tsp=tsp, sub=sub, nf=nf, n_items=n_items),
        grid_spec=grid_spec,
        out_shape=jax.ShapeDtypeStruct((n_items * ts, D), F32),
        compiler_params=_cparams(2, FFN_VMEM_LIMIT),
        name="ffn",
    )(item_e, item_nsub, row_tok, h2p, wgu, bgu.reshape(E, 1, F2), wd, bd.reshape(E, 1, D), sel)


def _combine_body(dest_hbm, ys_hbm, x1_ref, rg_ref, mod_ref, fg_ref, o_ref, idx_smem, buf_ref, sem, *, tm):
    i = pl.program_id(0)
    n = TOP_K * tm
    slot = lax.rem(i, 2)

    def fetch_rows(tile, sl):
        base = pl.multiple_of(sl * n, n)
        cp = pltpu.make_async_copy(dest_hbm.at[pl.ds(pl.multiple_of(tile * n, n), n)],
                                   idx_smem.at[pl.ds(base, n)], sem.at[2])
        cp.start()
        cp.wait()

        def issue8(r8, c):
            for k in range(8):
                r = r8 * 8 + k
                pltpu.make_async_copy(ys_hbm.at[idx_smem[base + r]], buf_ref.at[sl, r], sem.at[sl]).start()
            return c

        lax.fori_loop(0, n // 8, issue8, 0)

    @pl.when(i == 0)
    def _():
        fetch_rows(0, 0)

    @pl.when(i + 1 < pl.num_programs(0))
    def _():
        fetch_rows(i + 1, 1 - slot)

    pltpu.make_async_copy(ys_hbm.at[pl.ds(0, n)], buf_ref.at[slot], sem.at[slot]).wait()
    rg = rg_ref[...]
    y = jnp.zeros(x1_ref.shape, F32)
    for k in range(TOP_K):
        y = y + rg[:, k:k + 1] * buf_ref[slot, pl.ds(k * tm, tm), :]
    x2 = x1_ref[...] + mod_ref[5:6, :] * y
    o_ref[...] = _rms(x2, fg_ref[...])


def _combine(dest_km, y_sorted, x1, rg, mod, fg, tm):
    S, D = x1.shape
    return pl.pallas_call(
        functools.partial(_combine_body, tm=tm),
        grid=(S // tm,),
        in_specs=[pl.BlockSpec(memory_space=pl.ANY),
                  pl.BlockSpec(memory_space=pl.ANY),
                  pl.BlockSpec((tm, D), lambda i: (i, 0)),
                  pl.BlockSpec((tm, LANES), lambda i: (i, 0)),
                  pl.BlockSpec((6, D), lambda i: (0, 0)),
                  pl.BlockSpec((1, D), lambda i: (0, 0))],
        out_specs=pl.BlockSpec((tm, D), lambda i: (i, 0)),
        out_shape=jax.ShapeDtypeStruct((S, D), F32),
        scratch_shapes=[pltpu.SMEM((2 * TOP_K * tm,), I32),
                        pltpu.VMEM((2, TOP_K * tm, D), F32),
                        pltpu.SemaphoreType.DMA((3,))],
        compiler_params=_cparams(1),
        name="combine",
    )(dest_km, y_sorted, x1, rg, mod, fg.reshape(1, D))


def _routing_tables(ri, counts_f, n_experts, ts, tsp, sub, tm_c):
    S = ri.shape[0]
    counts = counts_f[0, :n_experts].astype(I32)
    nblk = (counts + ts - 1) // ts
    blk_end = jnp.cumsum(nblk)
    blk_start = blk_end - nblk
    e_idx = ri[:, :TOP_K]
    rank = ri[:, TOP_K:2 * TOP_K]
    dest = blk_start[e_idx] * ts + rank
    max_items = -(-(S * TOP_K) // ts) + n_experts
    tok = jnp.broadcast_to(jnp.arange(S, dtype=I32)[:, None], (S, TOP_K))
    slot = (dest // ts) * tsp + dest % ts
    row_tok = jnp.zeros((max_items * tsp,), I32).at[slot.reshape(-1)].set(
        tok.reshape(-1), unique_indices=True, mode='promise_in_bounds')
    item = jnp.arange(max_items, dtype=I32)
    n_items = blk_end[-1]
    item_c = jnp.minimum(item, n_items - 1)
    item_e = jnp.minimum(jnp.searchsorted(blk_end, item_c, side='right'), n_experts - 1).astype(I32)
    rows_left = counts[item_e] - (item_c - blk_start[item_e]) * ts
    nsub = jnp.clip((rows_left + sub - 1) // sub, 0, ts // sub)
    item_nsub = jnp.where(item < n_items, nsub, 0).astype(I32)
    dest_km = dest.reshape(S // tm_c, tm_c, TOP_K).transpose(0, 2, 1).reshape(-1)
    return item_e, item_nsub, row_tok, dest_km


def _forward(x, c, w_ada, b_ada, norm1_g, w_in, lambda_re, lambda_im, ssm_b_re, ssm_b_im,
             ssm_c_re, ssm_c_im, ssm_d, ssm_log_dt, w_glu, b_glu, attn_out_g, ssm_out_g,
             w_out, norm2_g, w_router, b_router, w_gate_up, b_gate_up, w_down, b_down, final_g,
             *, tiles):
    B, S, D = x.shape
    assert B == 1 and w_ada.shape[0] == 1
    A = attn_out_g.shape[1]
    Wd = ssm_out_g.shape[1]
    assert A == Wd and w_in.shape[2] == 3 * A + Wd
    E = w_router.shape[2]
    xs = x.reshape(S, D)

    mod = _adaln(c, w_ada[0], b_ada[0]).reshape(6, D)
    w_in_b = w_in[0].astype(BF16)
    qkv = _inproj(xs, norm1_g[0], mod, w_in_b[:, :3 * A], tiles['tm_in'], A, BF16, "inproj_qkv")
    u = _inproj(xs, norm1_g[0], mod, w_in_b[:, 3 * A:], tiles['tm_in'], Wd, F32, "inproj_u")
    branches = [_attn_branch(qkv, d, A) for d in ATTN_DILATIONS]
    o = [b[0] for b in branches]
    lse = [b[1] for b in branches]

    tables = _ssm_tables(lambda_re[0], lambda_im[0], ssm_b_re[0], ssm_b_im[0], ssm_c_re[0],
                         ssm_c_im[0], ssm_d[0], ssm_log_dt[0])
    ys = _ssm(u, tables)

    wr_pad = jnp.zeros((D, LANES), BF16).at[:, :E].set(w_router[0].astype(BF16))
    br_pad = jnp.full((1, LANES), NEG_BIG, F32).at[0, :E].set(b_router[0])
    x1, h2, ri, rg, counts = _mixout(
        xs, o, lse, ys, w_glu[0].astype(BF16), b_glu[0].reshape(1, Wd), attn_out_g[0].reshape(1, A),
        ssm_out_g[0].reshape(1, Wd), w_out[0].astype(BF16), mod, norm2_g[0].reshape(1, D),
        wr_pad, br_pad, tiles['tm_mix'])

    ts, sub, fc, tm_c = tiles['ts'], tiles['sub'], tiles['fc'], tiles['tm_c']
    tsp = -(-ts // SMEM_I32_TILE) * SMEM_I32_TILE
    item_e, item_nsub, row_tok, dest_km = _routing_tables(ri, counts, E, ts, tsp, sub, tm_c)
    y_sorted = _ffn(item_e, item_nsub, row_tok, h2, w_gate_up[0], b_gate_up[0], w_down[0], b_down[0],
                    ts, tsp, sub, fc)
    out = _combine(dest_km, y_sorted, x1, rg, mod, final_g, tm_c)
    return out.reshape(B, S, D)


TILES = dict(tm_in=512, tm_mix=256, ts=1280, sub=256, fc=256, tm_c=256)


def kernel(x, c, w_ada, b_ada, norm1_g, w_in, lambda_re, lambda_im, ssm_b_re, ssm_b_im, ssm_c_re, ssm_c_im, ssm_d, ssm_log_dt, w_glu, b_glu, attn_out_g, ssm_out_g, w_out, norm2_g, w_router, b_router, w_gate_up, b_gate_up, w_down, b_down, final_g):
    return _forward(x, c, w_ada, b_ada, norm1_g, w_in, lambda_re, lambda_im, ssm_b_re, ssm_b_im,
                    ssm_c_re, ssm_c_im, ssm_d, ssm_log_dt, w_glu, b_glu, attn_out_g, ssm_out_g,
                    w_out, norm2_g, w_router, b_router, w_gate_up, b_gate_up, w_down, b_down, final_g,
                    tiles=TILES)
```

```python
import functools
import math

import jax
import jax.numpy as jnp
from jax import lax
from jax.experimental import pallas as pl
from jax.experimental.pallas import tpu as pltpu

F32 = jnp.float32
BF16 = jnp.bfloat16
I32 = jnp.int32

HEAD_DIM = 64
ATTN_BLOCK = 128
ATTN_DILATIONS = (1, 4, 16)
SSM_P = 16
SSM_N = 64
SSM_L = 16
TOP_K = 4
SWIGLU_LIMIT = 7.0
SWIGLU_ALPHA = 1.702
NORM_EPS = 1e-6
LANES = 128
SSM_OCT = LANES // SSM_P
NEG_BIG = -1e30
VMEM_LIMIT = 56 * 1024 * 1024
FFN_VMEM_LIMIT = 60 * 1024 * 1024
SMEM_I32_TILE = 1024


def _cparams(n_axes, vmem=VMEM_LIMIT):
    return pltpu.CompilerParams(dimension_semantics=("arbitrary",) * n_axes,
                                vmem_limit_bytes=vmem)


def _rms(x, g):
    return x * lax.rsqrt(jnp.mean(x * x, axis=-1, keepdims=True) + NORM_EPS) * g


def _pack_bf16_pair(lo, hi):
    lb = lax.bitcast_convert_type(lo.astype(BF16).astype(F32), jnp.uint32)
    hb = lax.bitcast_convert_type(hi.astype(BF16).astype(F32), jnp.uint32)
    return (lb >> 16) | (hb & jnp.uint32(0xFFFF0000))


def _unpack_bf16_pair(w):
    return (lax.bitcast_convert_type(w << 16, F32),
            lax.bitcast_convert_type(w & jnp.uint32(0xFFFF0000), F32))


def _adaln_body(c_ref, w_ref, b_ref, o_ref):
    c = c_ref[...]
    ca = c * jax.nn.sigmoid(c)
    o_ref[...] = jnp.sum(ca * w_ref[...], axis=0, keepdims=True) + b_ref[...]


def _adaln(c, w, b):
    D, N = w.shape
    bn = next(b for b in (1024, 512, 256, LANES) if N % b == 0)
    return pl.pallas_call(
        _adaln_body,
        grid=(N // bn,),
        in_specs=[pl.BlockSpec((D, 1), lambda j: (0, 0)),
                  pl.BlockSpec((D, bn), lambda j: (0, j)),
                  pl.BlockSpec((1, bn), lambda j: (0, j))],
        out_specs=pl.BlockSpec((1, bn), lambda j: (0, j)),
        out_shape=jax.ShapeDtypeStruct((1, N), F32),
        compiler_params=_cparams(1),
        name="adaln",
    )(c.reshape(D, 1), w, b.reshape(1, N))


def _inproj_body(x_ref, g_ref, mod_ref, w_ref, o_ref, *, packed):
    h = _rms(x_ref[...], g_ref[...]) * (1.0 + mod_ref[1:2, :]) + mod_ref[0:1, :]
    y = jnp.dot(h.astype(BF16), w_ref[...], preferred_element_type=F32)
    if packed:
        half = y.shape[1] // 2
        o_ref[...] = _pack_bf16_pair(y[:, :half], y[:, half:])
    else:
        o_ref[...] = y


def _inproj(x, g, mod, w_bf16, tm, tn, packed, name):
    S, D = x.shape
    N = w_bf16.shape[1]
    to = tn // 2 if packed else tn
    return pl.pallas_call(
        functools.partial(_inproj_body, packed=packed),
        grid=(N // tn, S // tm),
        in_specs=[pl.BlockSpec((tm, D), lambda j, i: (i, 0)),
                  pl.BlockSpec((1, D), lambda j, i: (0, 0)),
                  pl.BlockSpec((6, D), lambda j, i: (0, 0)),
                  pl.BlockSpec((D, tn), lambda j, i: (0, j))],
        out_specs=pl.BlockSpec((tm, to), lambda j, i: (i, j)),
        out_shape=jax.ShapeDtypeStruct((S, N // tn * to), jnp.uint32 if packed else F32),
        compiler_params=_cparams(2),
        name=name,
    )(x, g.reshape(1, D), mod, w_bf16)


ROW_GROUP = 16


def _attn_body(src_hbm, o_hbm, l_hbm, buf, obuf, lbuf, sem_in, sem_out, *, dil, nb, n_heads):
    r = pl.program_id(0)
    n = pl.program_id(1)
    B = ATTN_BLOCK
    total = dil * nb
    g = r * nb + n
    nseg = 1 if dil == 1 else ROW_GROUP // dil
    seg = B // nseg
    PW = buf.shape[2] // 3

    def rows_of(ref, rr, nn, a):
        if dil == 1:
            return ref.at[pl.ds(nn * B, B), :]
        return ref.at[pl.ds(nn * seg, seg), dil * a + rr, :]

    def in_copies(rr, nn, slot):
        return [pltpu.make_async_copy(rows_of(src_hbm, rr, nn, a), buf.at[slot, pl.ds(a * seg, seg)],
                                      sem_in.at[slot]) for a in range(nseg)]

    def out_copies(rr, nn, slot):
        cps = []
        for a in range(nseg):
            cps.append(pltpu.make_async_copy(obuf.at[slot, pl.ds(a * seg, seg)], rows_of(o_hbm, rr, nn, a),
                                             sem_out.at[slot]))
            cps.append(pltpu.make_async_copy(lbuf.at[slot, pl.ds(a * seg, seg)], rows_of(l_hbm, rr, nn, a),
                                             sem_out.at[slot]))
        return cps

    cur, prv, nxt = lax.rem(g, 3), lax.rem(g + 2, 3), lax.rem(g + 1, 3)
    so = lax.rem(g, 2)

    @pl.when(g == 0)
    def _():
        buf[2] = jnp.zeros(buf.shape[1:], buf.dtype)
        for cp in in_copies(r, n, cur):
            cp.start()

    @pl.when(g + 1 < total)
    def _():
        wrap = n + 1 == nb
        for cp in in_copies(r + wrap.astype(I32), jnp.where(wrap, 0, n + 1), nxt):
            cp.start()

    for cp in in_copies(r, n, cur):
        cp.wait()

    @pl.when(g >= 2)
    def _():
        for cp in out_copies(r, n, so):
            cp.wait()

    def local_step(p):
        if nseg == 1:
            return p
        return nseg * (p & (seg - 1)) + (p >> (seg.bit_length() - 1))

    row = lax.broadcasted_iota(I32, (B, 2 * B), 0)
    col = lax.broadcasted_iota(I32, (B, 2 * B), 1)
    rowt = local_step(row)
    colt = (col & B) + local_step(col & (B - 1))
    mask = (colt >= rowt) & (colt <= rowt + B) & ((col >= B) | (n > 0))
    mask2 = jnp.concatenate([mask, mask], axis=0)
    lane = lax.broadcasted_iota(I32, (B, LANES), 1)
    first = lane < HEAD_DIM
    scale = HEAD_DIM ** -0.5
    cw, pw = buf[cur], buf[prv]
    lse_tile = jnp.zeros((B, LANES), F32)
    halves = []
    for hf, part in enumerate(zip(_unpack_bf16_pair(cw[:, :PW]),
                                  _unpack_bf16_pair(pw[:, PW:2 * PW]), _unpack_bf16_pair(cw[:, PW:2 * PW]),
                                  _unpack_bf16_pair(pw[:, 2 * PW:]), _unpack_bf16_pair(cw[:, 2 * PW:]))):
        qh, kp, kc, vp, vc = [a.astype(BF16) for a in part]
        pieces = []
        for i in range(PW // LANES):
            cs = slice(i * LANES, (i + 1) * LANES)
            q2 = qh[:, cs]
            lhs = jnp.concatenate([jnp.where(first, q2, 0), jnp.where(first, 0, q2)], axis=0)
            k2 = jnp.concatenate([kp[:, cs], kc[:, cs]], axis=0)
            v2 = jnp.concatenate([vp[:, cs], vc[:, cs]], axis=0)
            s = lax.dot_general(lhs, k2, (((1,), (1,)), ((), ())), preferred_element_type=F32) * scale
            s = jnp.where(mask2, s, NEG_BIG)
            m = jnp.max(s, axis=-1, keepdims=True)
            p = jnp.exp(s - m)
            l = jnp.sum(p, axis=-1, keepdims=True)
            pv = jnp.dot(p.astype(BF16), v2, preferred_element_type=F32) / l
            pieces.append(jnp.where(first, pv[:B], pv[B:]))
            lse = m + jnp.log(l)
            h0 = hf * (n_heads // 2) + 2 * i
            lse_tile = jnp.where(lane == h0, lse[:B], lse_tile)
            lse_tile = jnp.where(lane == h0 + 1, lse[B:], lse_tile)
        halves.append(jnp.concatenate(pieces, axis=1))
    obuf[so] = _pack_bf16_pair(halves[0], halves[1])
    lbuf[so] = lse_tile
    for cp in out_copies(r, n, so):
        cp.start()

    @pl.when(g == total - 1)
    def _():
        for cp in out_copies(r, n, so):
            cp.wait()
        if total > 1:
            for cp in out_copies(r, n, 1 - so):
                cp.wait()


def _attn_branch(qkv_p, dil, A):
    S, PW3 = qkv_p.shape
    nb = S // dil // ATTN_BLOCK
    view = (lambda a: a) if dil == 1 else (lambda a: a.reshape(S // ROW_GROUP, ROW_GROUP, a.shape[1]))
    oshape = lambda w, dt: jax.ShapeDtypeStruct((S, w) if dil == 1 else (S // ROW_GROUP, ROW_GROUP, w), dt)
    o, lse = pl.pallas_call(
        functools.partial(_attn_body, dil=dil, nb=nb, n_heads=A // HEAD_DIM),
        grid=(dil, nb),
        in_specs=[pl.BlockSpec(memory_space=pl.ANY)],
        out_specs=[pl.BlockSpec(memory_space=pl.ANY), pl.BlockSpec(memory_space=pl.ANY)],
        out_shape=[oshape(A // 2, jnp.uint32), oshape(LANES, F32)],
        scratch_shapes=[pltpu.VMEM((3, ATTN_BLOCK, PW3), jnp.uint32),
                        pltpu.VMEM((2, ATTN_BLOCK, A // 2), jnp.uint32),
                        pltpu.VMEM((2, ATTN_BLOCK, LANES), F32),
                        pltpu.SemaphoreType.DMA((3,)), pltpu.SemaphoreType.DMA((2,))],
        compiler_params=_cparams(2),
        name=f"attn_d{dil}",
    )(view(qkv_p))
    return o.reshape(S, A // 2), lse.reshape(S, LANES)


def _ssm_tables(lam_re, lam_im, b_re, b_im, c_re, c_im, d_skip, log_dt):
    G, N = lam_re.shape
    P, L, Q = SSM_P, SSM_L, SSM_OCT
    O = G // Q
    dt = jnp.exp(log_dt)[:, None]
    lr, li = lam_re, lam_im

    def powers(j):
        j = j.astype(F32)[:, None, None]
        mag = jnp.exp(j * (lr * dt))
        return mag * jnp.cos(j * (li * dt)), mag * jnp.sin(j * (li * dt))

    def c_times(pr, pi):
        return (c_re[None] * pr[:, :, None, :] - c_im[None] * pi[:, :, None, :],
                c_re[None] * pi[:, :, None, :] + c_im[None] * pr[:, :, None, :])

    one_r, one_i = powers(jnp.ones((1,), I32))
    den = lr * lr + li * li
    nr, ni = one_r[0] - 1.0, one_i[0]
    f_re = (nr * lr + ni * li) / den
    f_im = (ni * lr - nr * li) / den
    bb_re = f_re[..., None] * b_re - f_im[..., None] * b_im
    bb_im = f_re[..., None] * b_im + f_im[..., None] * b_re
    down = (L - 1) - jnp.arange(L, dtype=I32)
    dr, di = powers(down)
    cr_, ci_ = c_times(dr, di)
    kk = jnp.sum(cr_[..., None] * bb_re[None, :, None] - ci_[..., None] * bb_im[None, :, None], axis=3)
    skip = d_skip[:, :, None] * jnp.eye(P, dtype=F32)[None]
    kk = kk + jnp.where((down == 0)[:, None, None, None], skip[None], 0.0)
    kc = kk.reshape(L, O, Q, P, P).transpose(1, 0, 4, 2, 3).reshape(O, L, P, LANES)
    prr, pir = dr[:, :, :, None], di[:, :, :, None]
    w_r = (prr * bb_re[None] - pir * bb_im[None]).reshape(L, O, Q, N, P)
    w_i = (prr * bb_im[None] + pir * bb_re[None]).reshape(L, O, Q, N, P)
    wrc = w_r.transpose(1, 0, 4, 2, 3).reshape(O, L, P, Q * N)
    wic = w_i.transpose(1, 0, 4, 2, 3).reshape(O, L, P, Q * N)
    ur, ui = powers(1 + jnp.arange(L, dtype=I32))
    vr_, vi_ = c_times(ur, ui)
    vrc = vr_.reshape(L, O, Q, P, N).transpose(1, 4, 0, 2, 3).reshape(O, N, L * LANES)
    vic = -vi_.reshape(L, O, Q, P, N).transpose(1, 4, 0, 2, 3).reshape(O, N, L * LANES)
    pr_l, pi_l = ur[L - 1], ui[L - 1]
    avec = jnp.stack([pr_l.reshape(O, Q * N), pi_l.reshape(O, Q * N)], axis=1)
    return kc, wrc, wic, vrc, vic, avec


def _gelu_tanh(x):
    return 0.5 * x * (1.0 + jnp.tanh(math.sqrt(2.0 / math.pi) * (x + 0.044715 * (x * x * x))))


def _ssm_body(u3_hbm, kc_ref, wrc_ref, wic_ref, vrc_ref, vic_ref, a_ref, y3_hbm,
              ubuf, ucat, er_ref, ei_ref, xr_ref, xi_ref, e2_ref, wr_ref, wi_ref, vr_ref, vi_ref, sem):
    o = pl.program_id(0)
    L, P, N, Q = SSM_L, SSM_P, SSM_N, SSM_OCT
    NC, W = er_ref.shape
    cols = pl.ds(pl.multiple_of(o * LANES, LANES), LANES)
    loads = [pltpu.make_async_copy(u3_hbm.at[:, t, cols], ubuf.at[t], sem.at[0]) for t in range(L)]
    for cp in loads:
        cp.start()

    def diag(shape, row_shift, col_shift):
        r = lax.broadcasted_iota(I32, shape, 0) >> row_shift
        c = lax.broadcasted_iota(I32, shape, 1) >> col_shift
        return r == c

    lp, ln = P.bit_length() - 1, N.bit_length() - 1
    m_k = diag((LANES, LANES), lp, lp)
    m_w = diag((LANES, Q * N), lp, ln)
    for s in range(L):
        blk = slice(s * LANES, (s + 1) * LANES)
        right = jnp.where(m_k, jnp.tile(kc_ref[s], (Q, 1)), 0.0)
        left = jnp.where(m_k, jnp.tile(kc_ref[s + 1], (Q, 1)), 0.0) if s + 1 < L else jnp.zeros_like(right)
        e2_ref[blk, :] = jnp.concatenate([left, right], axis=1).astype(BF16)
        wr_ref[blk, :] = jnp.where(m_w, jnp.tile(wrc_ref[s], (Q, 1)), 0.0).astype(BF16)
        wi_ref[blk, :] = jnp.where(m_w, jnp.tile(wic_ref[s], (Q, 1)), 0.0).astype(BF16)
    m_v = diag((Q * N, LANES), ln, lp)
    for t in range(L):
        cb = slice(t * LANES, (t + 1) * LANES)
        vr_ref[:, cb] = jnp.where(m_v, jnp.tile(vrc_ref[:, cb], (Q, 1)), 0.0).astype(BF16)
        vi_ref[:, cb] = jnp.where(m_v, jnp.tile(vic_ref[:, cb], (Q, 1)), 0.0).astype(BF16)
    for cp in loads:
        cp.wait()
    for t in range(L):
        ucat[:, t * LANES:(t + 1) * LANES] = ubuf[t].astype(BF16)
    er_ref[...] = jnp.dot(ucat[...], wr_ref[...], preferred_element_type=F32)
    ei_ref[...] = jnp.dot(ucat[...], wi_ref[...], preferred_element_type=F32)
    ar, ai = a_ref[0:1, :], a_ref[1:2, :]

    def slab(j, carry):
        cr, ci = carry
        rows = pl.ds(pl.multiple_of(j * 8, 8), 8)
        e_r, e_i = er_ref[rows, :], ei_ref[rows, :]
        for s in range(8):
            xr_ref[pl.ds(j * 8 + s, 1), :] = cr
            xi_ref[pl.ds(j * 8 + s, 1), :] = ci
            cr, ci = (ar * cr - ai * ci + e_r[s:s + 1, :], ar * ci + ai * cr + e_i[s:s + 1, :])
        return cr, ci

    lax.fori_loop(0, NC // 8, slab, (jnp.zeros((1, W), F32), jnp.zeros((1, W), F32)))
    xrb = xr_ref[...].astype(BF16)
    xib = xi_ref[...].astype(BF16)
    for a in range(L // 2):
        k = (2 * a + 2) * LANES
        y = jnp.dot(ucat[:, :k], e2_ref[(L - 2 - 2 * a) * LANES:, :], preferred_element_type=F32)
        y = y + jnp.dot(xrb, vr_ref[:, 2 * a * LANES:(2 * a + 2) * LANES], preferred_element_type=F32)
        y = y + jnp.dot(xib, vi_ref[:, 2 * a * LANES:(2 * a + 2) * LANES], preferred_element_type=F32)
        y = _gelu_tanh(y)
        ubuf[2 * a] = y[:, :LANES]
        ubuf[2 * a + 1] = y[:, LANES:]
    stores = [pltpu.make_async_copy(ubuf.at[t], y3_hbm.at[:, t, cols], sem.at[1]) for t in range(L)]
    for cp in stores:
        cp.start()
    for cp in stores:
        cp.wait()


def _ssm(u, tables):
    S, Wd = u.shape
    L, N, Q = SSM_L, SSM_N, SSM_OCT
    NC = S // L
    O = Wd // LANES
    kc, wrc, wic, vrc, vic, avec = tables
    tab = lambda a: pl.BlockSpec((None,) + a.shape[1:], lambda o: (o,) + (0,) * (a.ndim - 1))
    y3 = pl.pallas_call(
        _ssm_body,
        grid=(O,),
        in_specs=[pl.BlockSpec(memory_space=pl.ANY)] + [tab(a) for a in (kc, wrc, wic, vrc, vic, avec)],
        out_specs=pl.BlockSpec(memory_space=pl.ANY),
        out_shape=jax.ShapeDtypeStruct((NC, L, Wd), F32),
        scratch_shapes=[pltpu.VMEM((L, NC, LANES), F32), pltpu.VMEM((NC, L * LANES), BF16)]
        + [pltpu.VMEM((NC, Q * N), F32)] * 4
        + [pltpu.VMEM((L * LANES, 2 * LANES), BF16), pltpu.VMEM((L * LANES, Q * N), BF16),
           pltpu.VMEM((L * LANES, Q * N), BF16), pltpu.VMEM((Q * N, L * LANES), BF16),
           pltpu.VMEM((Q * N, L * LANES), BF16), pltpu.SemaphoreType.DMA((2,))],
        compiler_params=_cparams(1),
        name="ssm",
    )(u.reshape(NC, L, Wd), kc, wrc, wic, vrc, vic, avec)
    return y3.reshape(S, Wd)


def _mixout_body(x_ref, o1_ref, o2_ref, o3_ref, l1_ref, l2_ref, l3_ref, ys_ref,
                 wglu_ref, bglu_ref, ag_ref, sg_ref, wout_ref, mod_ref, n2g_ref, wr_ref, br_ref,
                 x1_ref, h2_ref, ri_ref, rg_ref, cnt_ref, carry_ref, *, n_heads):
    i = pl.program_id(0)
    TM = x_ref.shape[0]

    @pl.when(i == 0)
    def _():
        carry_ref[...] = jnp.zeros_like(carry_ref)

    l1, l2, l3 = l1_ref[...], l2_ref[...], l3_ref[...]
    lm = jnp.maximum(jnp.maximum(l1, l2), l3)
    e1, e2, e3 = jnp.exp(l1 - lm), jnp.exp(l2 - lm), jnp.exp(l3 - lm)
    inv = 1.0 / (e1 + e2 + e3)
    w1, w2, w3 = e1 * inv, e2 * inv, e3 * inv
    o1, o2, o3 = [jnp.concatenate(_unpack_bf16_pair(ref[...]), axis=1) for ref in (o1_ref, o2_ref, o3_ref)]
    pieces = []
    for h in range(n_heads):
        hs = slice(h * HEAD_DIM, (h + 1) * HEAD_DIM)
        pieces.append(w1[:, h:h + 1] * o1[:, hs] + w2[:, h:h + 1] * o2[:, hs] + w3[:, h:h + 1] * o3[:, hs])
    ya = _rms(jnp.concatenate(pieces, axis=1), ag_ref[...])
    ys = ys_ref[...]
    z = jnp.dot(ys.astype(BF16), wglu_ref[...], preferred_element_type=F32) + bglu_ref[...]
    yg = _rms(ys * jax.nn.sigmoid(z), sg_ref[...])
    ymix = jnp.concatenate([ya, yg], axis=1).astype(BF16)
    x1 = x_ref[...] + mod_ref[2:3, :] * jnp.dot(ymix, wout_ref[...], preferred_element_type=F32)
    x1_ref[...] = x1
    h2 = _rms(x1, n2g_ref[...]) * (1.0 + mod_ref[4:5, :]) + mod_ref[3:4, :]
    half = h2.shape[1] // 2
    h2_ref[...] = _pack_bf16_pair(h2[:, :half], h2[:, half:])
    logits = jnp.dot(h2.astype(BF16), wr_ref[...], preferred_element_type=F32) + br_ref[...]
    lane = lax.broadcasted_iota(I32, (TM, LANES), 1).astype(F32)
    work = logits
    vals, idxs = [], []
    for _ in range(TOP_K):
        m = jnp.max(work, axis=1, keepdims=True)
        idx = jnp.min(jnp.where(work == m, lane, float(LANES)), axis=1, keepdims=True)
        vals.append(m)
        idxs.append(idx)
        work = jnp.where(lane == idx, -3e38, work)
    ex = [jnp.exp(v - vals[0]) for v in vals]
    den = ex[0] + ex[1] + ex[2] + ex[3]
    onehot = jnp.zeros((TM, LANES), F32)
    for idx in idxs:
        onehot = onehot + jnp.where(lane == idx, 1.0, 0.0)
    r_i = lax.broadcasted_iota(I32, (TM, TM), 0)
    c_i = lax.broadcasted_iota(I32, (TM, TM), 1)
    tri = jnp.where(c_i < r_i, 1.0, 0.0).astype(BF16)
    before = jnp.dot(tri, onehot.astype(BF16), preferred_element_type=F32) + carry_ref[...]
    ri = jnp.zeros((TM, LANES), F32)
    rg = jnp.zeros((TM, LANES), F32)
    for k in range(TOP_K):
        rank = jnp.sum(jnp.where(lane == idxs[k], before, 0.0), axis=1, keepdims=True)
        ri = jnp.where(lane == float(k), idxs[k], ri)
        ri = jnp.where(lane == float(TOP_K + k), rank, ri)
        rg = jnp.where(lane == float(k), ex[k] / den, rg)
    ri_ref[...] = ri.astype(I32)
    rg_ref[...] = rg
    carry_ref[...] = carry_ref[...] + jnp.sum(onehot, axis=0, keepdims=True)
    cnt_ref[...] = carry_ref[...]


def _mixout(x, o, lse, ys, wglu, bglu, ag, sg, wout, mod, n2g, wr_pad, br_pad, tm):
    S, D = x.shape
    A = 2 * o[0].shape[1]
    Wd = ys.shape[1]
    row = lambda w: pl.BlockSpec((tm, w), lambda i: (i, 0))
    full = lambda a: pl.BlockSpec(a.shape, lambda i: (0,) * a.ndim)
    ins = [x, o[0], o[1], o[2], lse[0], lse[1], lse[2], ys,
           wglu, bglu, ag, sg, wout, mod, n2g, wr_pad, br_pad]
    in_specs = [row(D), row(A // 2), row(A // 2), row(A // 2), row(LANES), row(LANES), row(LANES), row(Wd)]
    in_specs += [full(a) for a in ins[8:]]
    return pl.pallas_call(
        functools.partial(_mixout_body, n_heads=A // HEAD_DIM),
        grid=(S // tm,),
        in_specs=in_specs,
        out_specs=[row(D), row(D // 2), row(LANES), row(LANES), pl.BlockSpec((1, LANES), lambda i: (0, 0))],
        out_shape=[jax.ShapeDtypeStruct((S, D), F32), jax.ShapeDtypeStruct((S, D // 2), jnp.uint32),
                   jax.ShapeDtypeStruct((S, LANES), I32), jax.ShapeDtypeStruct((S, LANES), F32),
                   jax.ShapeDtypeStruct((1, LANES), F32)],
        scratch_shapes=[pltpu.VMEM((1, LANES), F32)],
        compiler_params=_cparams(1),
        name="mixout",
    )(*ins)


def _ffn_body(ie_ref, ns_ref, tok_hbm, h_hbm, wgu_ref, bgu_ref, wd_ref, bd_ref, sel_ref, o_ref,
              idx_smem, xp_ref, xb_ref, wgu_s, wd_s, sem, *, ts, tsp, sub, nf, n_items):
    i = pl.program_id(0)
    j = pl.program_id(1)
    nsub = ns_ref[i]
    slot = lax.rem(i, 2)
    nxt = jnp.minimum(i + 1, n_items - 1)
    nsub_nxt = jnp.where(i + 1 < n_items, ns_ref[nxt], 0)

    def fetch_rows(item, item_nsub, sl):
        base = pl.multiple_of(sl * tsp, tsp)
        cp = pltpu.make_async_copy(tok_hbm.at[pl.ds(pl.multiple_of(item * tsp, tsp), tsp)],
                                   idx_smem.at[pl.ds(base, tsp)], sem.at[2])
        cp.start()
        cp.wait()

        def issue8(r8, c):
            for k in range(8):
                r = r8 * 8 + k
                pltpu.make_async_copy(h_hbm.at[idx_smem[base + r]], xp_ref.at[sl, r], sem.at[sl]).start()
            return c

        lax.fori_loop(0, item_nsub * (sub // 8), issue8, 0)

    @pl.when(j == 0)
    def _():
        o_ref[...] = jnp.broadcast_to(bd_ref[...], o_ref.shape)

    @pl.when((i == 0) & (j == 0))
    def _():
        fetch_rows(0, nsub, 0)

    @pl.when((nsub > 0) & (j == 0))
    def _():
        def drain(s, c):
            pltpu.make_async_copy(h_hbm.at[pl.ds(0, sub)], xp_ref.at[slot, pl.ds(0, sub)], sem.at[slot]).wait()
            return c

        lax.fori_loop(0, nsub, drain, 0)

        def unpack(s, c):
            rows = pl.ds(pl.multiple_of(s * sub, sub), sub)
            lo, hi = _unpack_bf16_pair(xp_ref[slot, rows, :])
            xb_ref[rows, :] = jnp.concatenate([lo, hi], axis=1).astype(BF16)
            return c

        lax.fori_loop(0, nsub, unpack, 0)

    @pl.when((j == 1) & (nsub_nxt > 0))
    def _():
        fetch_rows(nxt, nsub_nxt, 1 - slot)

    @pl.when(nsub > 0)
    def _():
        wgu = wgu_ref[...].astype(BF16)
        wd = wd_ref[...].astype(BF16)
        wgu_s[...] = wgu
        wd_s[...] = wd
        bgu = bgu_ref[...]
        n2 = wgu.shape[1]

        def gate_up(s, w, n=1):
            rows = pl.ds(pl.multiple_of(s * sub, sub), n * sub)
            gu = jnp.dot(xb_ref[rows, :], w, preferred_element_type=F32) + bgu
            glu = jnp.minimum(gu, SWIGLU_LIMIT)
            a = glu * jax.nn.sigmoid(SWIGLU_ALPHA * glu)
            b = jnp.clip(gu, -SWIGLU_LIMIT, SWIGLU_LIMIT) + 1.0
            prod = (a * pltpu.roll(b, n2 - 1, axis=1)).astype(BF16)
            return jnp.dot(prod, sel_ref[...], preferred_element_type=F32).astype(BF16)

        def down(s, act, w, n=1):
            rows = pl.ds(pl.multiple_of(s * sub, sub), n * sub)
            o_ref[rows, :] = o_ref[rows, :] + jnp.dot(act, w, preferred_element_type=F32)

        npair = lax.shift_right_logical(nsub, 1)

        @pl.when(npair > 0)
        def _():
            def pair(p, act):
                nxt_act = gate_up(2 * p, wgu_s[...], 2)
                down(2 * p - 2, act, wd_s[...], 2)
                return nxt_act

            act = lax.fori_loop(1, npair, pair, gate_up(0, wgu, 2))
            down(2 * npair - 2, act, wd_s[...], 2)

        @pl.when(nsub > 2 * npair)
        def _():
            down(nsub - 1, gate_up(nsub - 1, wgu_s[...]), wd_s[...])


def _ffn(item_e, item_nsub, row_tok, h2p, wgu, bgu, wd, bd, ts, tsp, sub, fc):
    E, D, F2 = wgu.shape
    F = F2 // 2
    nf = F // fc
    n_items = item_e.shape[0]
    sel = (jnp.arange(2 * fc)[:, None] == 2 * jnp.arange(fc)[None, :]).astype(BF16)

    def chunk(i, j, ie, ns):
        return jnp.where(ns[i] > 0, j, nf - 1)

    grid_spec = pltpu.PrefetchScalarGridSpec(
        num_scalar_prefetch=2,
        grid=(n_items, nf),
        in_specs=[pl.BlockSpec(memory_space=pl.ANY),
                  pl.BlockSpec(memory_space=pl.ANY),
                  pl.BlockSpec((None, D, 2 * fc), lambda i, j, ie, ns: (ie[i], 0, chunk(i, j, ie, ns))),
                  pl.BlockSpec((None, 1, 2 * fc), lambda i, j, ie, ns: (ie[i], 0, chunk(i, j, ie, ns))),
                  pl.BlockSpec((None, fc, D), lambda i, j, ie, ns: (ie[i], chunk(i, j, ie, ns), 0)),
                  pl.BlockSpec((None, 1, D), lambda i, j, ie, ns: (ie[i], 0, 0)),
                  pl.BlockSpec((2 * fc, fc), lambda i, j, ie, ns: (0, 0))],
        out_specs=pl.BlockSpec((ts, D), lambda i, j, ie, ns: (i, 0)),
        scratch_shapes=[pltpu.SMEM((2 * tsp,), I32),
                        pltpu.VMEM((2, ts, D // 2), jnp.uint32),
                        pltpu.VMEM((ts, D), BF16),
                        pltpu.VMEM((D, 2 * fc), BF16),
                        pltpu.VMEM((fc, D), BF16),
                        pltpu.SemaphoreType.DMA((3,))],
    )
    return pl.pallas_call(
        functools.partial(_ffn_body, ts=ts, tsp=tsp, sub=sub, nf=nf, n_items=n_items),
        grid_spec=grid_spec,
        out_shape=jax.ShapeDtypeStruct((n_items * ts, D), F32),
        compiler_params=_cparams(2, FFN_VMEM_LIMIT),
        name="ffn",
    )(item_e, item_nsub, row_tok, h2p, wgu, bgu.reshape(E, 1, F2), wd, bd.reshape(E, 1, D), sel)


def _combine_body(dest_hbm, ys_hbm, x1_ref, rg_ref, mod_ref, fg_ref, o_ref, idx_smem, buf_ref, sem, *, tm):
    i = pl.program_id(0)
    n = TOP_K * tm
    slot = lax.rem(i, 2)

    def fetch_rows(tile, sl):
        base = pl.multiple_of(sl * n, n)
        cp = pltpu.make_async_copy(dest_hbm.at[pl.ds(pl.multiple_of(tile * n, n), n)],
                                   idx_smem.at[pl.ds(base, n)], sem.at[2])
        cp.start()
        cp.wait()

        def issue8(r8, c):
            for k in range(8):
                r = r8 * 8 + k
                pltpu.make_async_copy(ys_hbm.at[idx_smem[base + r]], buf_ref.at[sl, r], sem.at[sl]).start()
            return c

        lax.fori_loop(0, n // 8, issue8, 0)

    @pl.when(i == 0)
    def _():
        fetch_rows(0, 0)

    @pl.when(i + 1 < pl.num_programs(0))
    def _():
        fetch_rows(i + 1, 1 - slot)

    pltpu.make_async_copy(ys_hbm.at[pl.ds(0, n)], buf_ref.at[slot], sem.at[slot]).wait()
    rg = rg_ref[...]
    y = jnp.zeros(x1_ref.shape, F32)
    for k in range(TOP_K):
        y = y + rg[:, k:k + 1] * buf_ref[slot, pl.ds(k * tm, tm), :]
    x2 = x1_ref[...] + mod_ref[5:6, :] * y
    o_ref[...] = _rms(x2, fg_ref[...])


def _combine(dest_km, y_sorted, x1, rg, mod, fg, tm):
    S, D = x1.shape
    return pl.pallas_call(
        functools.partial(_combine_body, tm=tm),
        grid=(S // tm,),
        in_specs=[pl.BlockSpec(memory_space=pl.ANY),
                  pl.BlockSpec(memory_space=pl.ANY),
                  pl.BlockSpec((tm, D), lambda i: (i, 0)),
                  pl.BlockSpec((tm, LANES), lambda i: (i, 0)),
                  pl.BlockSpec((6, D), lambda i: (0, 0)),
                  pl.BlockSpec((1, D), lambda i: (0, 0))],
        out_specs=pl.BlockSpec((tm, D), lambda i: (i, 0)),
        out_shape=jax.ShapeDtypeStruct((S, D), F32),
        scratch_shapes=[pltpu.SMEM((2 * TOP_K * tm,), I32),
                        pltpu.VMEM((2, TOP_K * tm, D), F32),
                        pltpu.SemaphoreType.DMA((3,))],
        compiler_params=_cparams(1),
        name="combine",
    )(dest_km, y_sorted, x1, rg, mod, fg.reshape(1, D))


def _routing_tables(ri, counts_f, n_experts, ts, tsp, sub, tm_c):
    S = ri.shape[0]
    counts = counts_f[0, :n_experts].astype(I32)
    nblk = (counts + ts - 1) // ts
    blk_end = jnp.cumsum(nblk)
    blk_start = blk_end - nblk
    e_idx = ri[:, :TOP_K]
    rank = ri[:, TOP_K:2 * TOP_K]
    dest = blk_start[e_idx] * ts + rank
    max_items = -(-(S * TOP_K) // ts) + n_experts
    tok = jnp.broadcast_to(jnp.arange(S, dtype=I32)[:, None], (S, TOP_K))
    slot = (dest // ts) * tsp + dest % ts
    row_tok = jnp.zeros((max_items * tsp,), I32).at[slot.reshape(-1)].set(
        tok.reshape(-1), unique_indices=True, mode='promise_in_bounds')
    item = jnp.arange(max_items, dtype=I32)
    n_items = blk_end[-1]
    item_c = jnp.minimum(item, n_items - 1)
    item_e = jnp.minimum(jnp.searchsorted(blk_end, item_c, side='right'), n_experts - 1).astype(I32)
    rows_left = counts[item_e] - (item_c - blk_start[item_e]) * ts
    nsub = jnp.clip((rows_left + sub - 1) // sub, 0, ts // sub)
    item_nsub = jnp.where(item < n_items, nsub, 0).astype(I32)
    dest_km = dest.reshape(S // tm_c, tm_c, TOP_K).transpose(0, 2, 1).reshape(-1)
    return item_e, item_nsub, row_tok, dest_km


def _forward(x, c, w_ada, b_ada, norm1_g, w_in, lambda_re, lambda_im, ssm_b_re, ssm_b_im,
             ssm_c_re, ssm_c_im, ssm_d, ssm_log_dt, w_glu, b_glu, attn_out_g, ssm_out_g,
             w_out, norm2_g, w_router, b_router, w_gate_up, b_gate_up, w_down, b_down, final_g,
             *, tiles):
    B, S, D = x.shape
    assert B == 1 and w_ada.shape[0] == 1
    A = attn_out_g.shape[1]
    Wd = ssm_out_g.shape[1]
    assert A == Wd and w_in.shape[2] == 3 * A + Wd
    E = w_router.shape[2]
    xs = x.reshape(S, D)

    mod = _adaln(c, w_ada[0], b_ada[0]).reshape(6, D)
    w_in_b = w_in[0].astype(BF16)
    qkv = _inproj(xs, norm1_g[0], mod, w_in_b[:, :3 * A], tiles['tm_in'], A, True, "inproj_qkv")
    u = _inproj(xs, norm1_g[0], mod, w_in_b[:, 3 * A:], tiles['tm_in'], Wd, False, "inproj_u")
    branches = [_attn_branch(qkv, d, A) for d in ATTN_DILATIONS]
    o = [b[0] for b in branches]
    lse = [b[1] for b in branches]

    tables = _ssm_tables(lambda_re[0], lambda_im[0], ssm_b_re[0], ssm_b_im[0], ssm_c_re[0],
                         ssm_c_im[0], ssm_d[0], ssm_log_dt[0])
    ys = _ssm(u, tables)

    wr_pad = jnp.zeros((D, LANES), BF16).at[:, :E].set(w_router[0].astype(BF16))
    br_pad = jnp.full((1, LANES), NEG_BIG, F32).at[0, :E].set(b_router[0])
    x1, h2, ri, rg, counts = _mixout(
        xs, o, lse, ys, w_glu[0].astype(BF16), b_glu[0].reshape(1, Wd), attn_out_g[0].reshape(1, A),
        ssm_out_g[0].reshape(1, Wd), w_out[0].astype(BF16), mod, norm2_g[0].reshape(1, D),
        wr_pad, br_pad, tiles['tm_mix'])

    ts, sub, fc, tm_c = tiles['ts'], tiles['sub'], tiles['fc'], tiles['tm_c']
    tsp = -(-ts // SMEM_I32_TILE) * SMEM_I32_TILE
    item_e, item_nsub, row_tok, dest_km = _routing_tables(ri, counts, E, ts, tsp, sub, tm_c)
    y_sorted = _ffn(item_e, item_nsub, row_tok, h2, w_gate_up[0], b_gate_up[0], w_down[0], b_down[0],
                    ts, tsp, sub, fc)
    out = _combine(dest_km, y_sorted, x1, rg, mod, final_g, tm_c)
    return out.reshape(B, S, D)


TILES = dict(tm_in=512, tm_mix=256, ts=1280, sub=256, fc=256, tm_c=256)


def kernel(x, c, w_ada, b_ada, norm1_g, w_in, lambda_re, lambda_im, ssm_b_re, ssm_b_im, ssm_c_re, ssm_c_im, ssm_d, ssm_log_dt, w_glu, b_glu, attn_out_g, ssm_out_g, w_out, norm2_g, w_router, b_router, w_gate_up, b_gate_up, w_down, b_down, final_g):
    return _forward(x, c, w_ada, b_ada, norm1_g, w_in, lambda_re, lambda_im, ssm_b_re, ssm_b_im,
                    ssm_c_re, ssm_c_im, ssm_d, ssm_log_dt, w_glu, b_glu, attn_out_g, ssm_out_g,
                    w_out, norm2_g, w_router, b_router, w_gate_up, b_gate_up, w_down, b_down, final_g,
                    tiles=TILES)
```

```python
import functools
import math

import jax
import jax.numpy as jnp
from jax import lax
from jax.experimental import pallas as pl
from jax.experimental.pallas import tpu as pltpu

F32 = jnp.float32
BF16 = jnp.bfloat16
I32 = jnp.int32

HEAD_DIM = 64
ATTN_BLOCK = 128
ATTN_DILATIONS = (1, 4, 16)
SSM_P = 16
SSM_N = 64
SSM_L = 16
TOP_K = 4
SWIGLU_LIMIT = 7.0
SWIGLU_ALPHA = 1.702
NORM_EPS = 1e-6
LANES = 128
SSM_OCT = LANES // SSM_P
NEG_BIG = -1e30
VMEM_LIMIT = 56 * 1024 * 1024
FFN_VMEM_LIMIT = 60 * 1024 * 1024
SMEM_I32_TILE = 1024


def _cparams(n_axes, vmem=VMEM_LIMIT):
    return pltpu.CompilerParams(dimension_semantics=("arbitrary",) * n_axes,
                                vmem_limit_bytes=vmem)


def _rms(x, g):
    return x * lax.rsqrt(jnp.mean(x * x, axis=-1, keepdims=True) + NORM_EPS) * g


def _pack_bf16_pair(lo, hi):
    lb = lax.bitcast_convert_type(lo.astype(BF16).astype(F32), jnp.uint32)
    hb = lax.bitcast_convert_type(hi.astype(BF16).astype(F32), jnp.uint32)
    return (lb >> 16) | (hb & jnp.uint32(0xFFFF0000))


def _unpack_bf16_pair(w):
    return (lax.bitcast_convert_type(w << 16, F32),
            lax.bitcast_convert_type(w & jnp.uint32(0xFFFF0000), F32))


def _adaln_body(c_ref, w_ref, b_ref, o_ref):
    c = c_ref[...]
    ca = c * jax.nn.sigmoid(c)
    o_ref[...] = jnp.sum(ca * w_ref[...], axis=0, keepdims=True) + b_ref[...]


def _adaln(c, w, b):
    D, N = w.shape
    bn = next(b for b in (1024, 512, 256, LANES) if N % b == 0)
    return pl.pallas_call(
        _adaln_body,
        grid=(N // bn,),
        in_specs=[pl.BlockSpec((D, 1), lambda j: (0, 0)),
                  pl.BlockSpec((D, bn), lambda j: (0, j)),
                  pl.BlockSpec((1, bn), lambda j: (0, j))],
        out_specs=pl.BlockSpec((1, bn), lambda j: (0, j)),
        out_shape=jax.ShapeDtypeStruct((1, N), F32),
        compiler_params=_cparams(1),
        name="adaln",
    )(c.reshape(D, 1), w, b.reshape(1, N))


def _inproj_body(x_ref, g_ref, mod_ref, w_ref, o_ref, *, packed):
    h = _rms(x_ref[...], g_ref[...]) * (1.0 + mod_ref[1:2, :]) + mod_ref[0:1, :]
    y = jnp.dot(h.astype(BF16), w_ref[...], preferred_element_type=F32)
    if packed:
        half = y.shape[1] // 2
        o_ref[...] = _pack_bf16_pair(y[:, :half], y[:, half:])
    else:
        o_ref[...] = y


def _inproj(x, g, mod, w_bf16, tm, tn, packed, name):
    S, D = x.shape
    N = w_bf16.shape[1]
    to = tn // 2 if packed else tn
    return pl.pallas_call(
        functools.partial(_inproj_body, packed=packed),
        grid=(N // tn, S // tm),
        in_specs=[pl.BlockSpec((tm, D), lambda j, i: (i, 0)),
                  pl.BlockSpec((1, D), lambda j, i: (0, 0)),
                  pl.BlockSpec((6, D), lambda j, i: (0, 0)),
                  pl.BlockSpec((D, tn), lambda j, i: (0, j))],
        out_specs=pl.BlockSpec((tm, to), lambda j, i: (i, j)),
        out_shape=jax.ShapeDtypeStruct((S, N // tn * to), jnp.uint32 if packed else F32),
        compiler_params=_cparams(2),
        name=name,
    )(x, g.reshape(1, D), mod, w_bf16)


ROW_GROUP = 16


def _attn_body(src_hbm, o_hbm, l_hbm, buf, obuf, lbuf, sem_in, sem_out, *, dil, nb, n_heads):
    r = pl.program_id(0)
    n = pl.program_id(1)
    B = ATTN_BLOCK
    total = dil * nb
    g = r * nb + n
    nseg = 1 if dil == 1 else ROW_GROUP // dil
    seg = B // nseg
    PW = buf.shape[2] // 3

    def rows_of(ref, rr, nn, a):
        if dil == 1:
            return ref.at[pl.ds(nn * B, B), :]
        return ref.at[pl.ds(nn * seg, seg), dil * a + rr, :]

    def in_copies(rr, nn, slot):
        return [pltpu.make_async_copy(rows_of(src_hbm, rr, nn, a), buf.at[slot, pl.ds(a * seg, seg)],
                                      sem_in.at[slot]) for a in range(nseg)]

    def out_copies(rr, nn, slot):
        cps = []
        for a in range(nseg):
            cps.append(pltpu.make_async_copy(obuf.at[slot, pl.ds(a * seg, seg)], rows_of(o_hbm, rr, nn, a),
                                             sem_out.at[slot]))
            cps.append(pltpu.make_async_copy(lbuf.at[slot, pl.ds(a * seg, seg)], rows_of(l_hbm, rr, nn, a),
                                             sem_out.at[slot]))
        return cps

    cur, prv, nxt = lax.rem(g, 3), lax.rem(g + 2, 3), lax.rem(g + 1, 3)
    so = lax.rem(g, 2)

    @pl.when(g == 0)
    def _():
        buf[2] = jnp.zeros(buf.shape[1:], buf.dtype)
        for cp in in_copies(r, n, cur):
            cp.start()

    @pl.when(g + 1 < total)
    def _():
        wrap = n + 1 == nb
        for cp in in_copies(r + wrap.astype(I32), jnp.where(wrap, 0, n + 1), nxt):
            cp.start()

    for cp in in_copies(r, n, cur):
        cp.wait()

    @pl.when(g >= 2)
    def _():
        for cp in out_copies(r, n, so):
            cp.wait()

    def local_step(p):
        if nseg == 1:
            return p
        return nseg * (p & (seg - 1)) + (p >> (seg.bit_length() - 1))

    row = lax.broadcasted_iota(I32, (B, 2 * B), 0)
    col = lax.broadcasted_iota(I32, (B, 2 * B), 1)
    rowt = local_step(row)
    colt = (col & B) + local_step(col & (B - 1))
    mask = (colt >= rowt) & (colt <= rowt + B) & ((col >= B) | (n > 0))
    mask2 = jnp.concatenate([mask, mask], axis=0)
    lane = lax.broadcasted_iota(I32, (B, LANES), 1)
    first = lane < HEAD_DIM
    scale = HEAD_DIM ** -0.5
    cw, pw = buf[cur], buf[prv]
    lse_tile = jnp.zeros((B, LANES), F32)
    halves = []
    for hf, part in enumerate(zip(_unpack_bf16_pair(cw[:, :PW]),
                                  _unpack_bf16_pair(pw[:, PW:2 * PW]), _unpack_bf16_pair(cw[:, PW:2 * PW]),
                                  _unpack_bf16_pair(pw[:, 2 * PW:]), _unpack_bf16_pair(cw[:, 2 * PW:]))):
        qh, kp, kc, vp, vc = [a.astype(BF16) for a in part]
        pieces = []
        for i in range(PW // LANES):
            cs = slice(i * LANES, (i + 1) * LANES)
            q2 = qh[:, cs]
            lhs = jnp.concatenate([jnp.where(first, q2, 0), jnp.where(first, 0, q2)], axis=0)
            k2 = jnp.concatenate([kp[:, cs], kc[:, cs]], axis=0)
            v2 = jnp.concatenate([vp[:, cs], vc[:, cs]], axis=0)
            s = lax.dot_general(lhs, k2, (((1,), (1,)), ((), ())), preferred_element_type=F32) * scale
            s = jnp.where(mask2, s, NEG_BIG)
            m = jnp.max(s, axis=-1, keepdims=True)
            p = jnp.exp(s - m)
            l = jnp.sum(p, axis=-1, keepdims=True)
            pv = jnp.dot(p.astype(BF16), v2, preferred_element_type=F32) / l
            pieces.append(jnp.where(first, pv[:B], pv[B:]))
            lse = m + jnp.log(l)
            h0 = hf * (n_heads // 2) + 2 * i
            lse_tile = jnp.where(lane == h0, lse[:B], lse_tile)
            lse_tile = jnp.where(lane == h0 + 1, lse[B:], lse_tile)
        halves.append(jnp.concatenate(pieces, axis=1))
    obuf[so] = _pack_bf16_pair(halves[0], halves[1])
    lbuf[so] = lse_tile
    for cp in out_copies(r, n, so):
        cp.start()

    @pl.when(g == total - 1)
    def _():
        for cp in out_copies(r, n, so):
            cp.wait()
        if total > 1:
            for cp in out_copies(r, n, 1 - so):
                cp.wait()


def _attn_branch(qkv_p, dil, A):
    S, PW3 = qkv_p.shape
    nb = S // dil // ATTN_BLOCK
    view = (lambda a: a) if dil == 1 else (lambda a: a.reshape(S // ROW_GROUP, ROW_GROUP, a.shape[1]))
    oshape = lambda w, dt: jax.ShapeDtypeStruct((S, w) if dil == 1 else (S // ROW_GROUP, ROW_GROUP, w), dt)
    o, lse = pl.pallas_call(
        functools.partial(_attn_body, dil=dil, nb=nb, n_heads=A // HEAD_DIM),
        grid=(dil, nb),
        in_specs=[pl.BlockSpec(memory_space=pl.ANY)],
        out_specs=[pl.BlockSpec(memory_space=pl.ANY), pl.BlockSpec(memory_space=pl.ANY)],
        out_shape=[oshape(A // 2, jnp.uint32), oshape(LANES, F32)],
        scratch_shapes=[pltpu.VMEM((3, ATTN_BLOCK, PW3), jnp.uint32),
                        pltpu.VMEM((2, ATTN_BLOCK, A // 2), jnp.uint32),
                        pltpu.VMEM((2, ATTN_BLOCK, LANES), F32),
                        pltpu.SemaphoreType.DMA((3,)), pltpu.SemaphoreType.DMA((2,))],
        compiler_params=_cparams(2),
        name=f"attn_d{dil}",
    )(view(qkv_p))
    return o.reshape(S, A // 2), lse.reshape(S, LANES)


def _ssm_tables(lam_re, lam_im, b_re, b_im, c_re, c_im, d_skip, log_dt):
    G, N = lam_re.shape
    P, L, Q = SSM_P, SSM_L, SSM_OCT
    O = G // Q
    dt = jnp.exp(log_dt)[:, None]
    lr, li = lam_re, lam_im

    def powers(j):
        j = j.astype(F32)[:, None, None]
        mag = jnp.exp(j * (lr * dt))
        return mag * jnp.cos(j * (li * dt)), mag * jnp.sin(j * (li * dt))

    def c_times(pr, pi):
        return (c_re[None] * pr[:, :, None, :] - c_im[None] * pi[:, :, None, :],
                c_re[None] * pi[:, :, None, :] + c_im[None] * pr[:, :, None, :])

    one_r, one_i = powers(jnp.ones((1,), I32))
    den = lr * lr + li * li
    nr, ni = one_r[0] - 1.0, one_i[0]
    f_re = (nr * lr + ni * li) / den
    f_im = (ni * lr - nr * li) / den
    bb_re = f_re[..., None] * b_re - f_im[..., None] * b_im
    bb_im = f_re[..., None] * b_im + f_im[..., None] * b_re
    down = (L - 1) - jnp.arange(L, dtype=I32)
    dr, di = powers(down)
    cr_, ci_ = c_times(dr, di)
    kk = jnp.sum(cr_[..., None] * bb_re[None, :, None] - ci_[..., None] * bb_im[None, :, None], axis=3)
    skip = d_skip[:, :, None] * jnp.eye(P, dtype=F32)[None]
    kk = kk + jnp.where((down == 0)[:, None, None, None], skip[None], 0.0)
    kc = kk.reshape(L, O, Q, P, P).transpose(1, 0, 4, 2, 3).reshape(O, L, P, LANES)
    prr, pir = dr[:, :, :, None], di[:, :, :, None]
    w_r = (prr * bb_re[None] - pir * bb_im[None]).reshape(L, O, Q, N, P)
    w_i = (prr * bb_im[None] + pir * bb_re[None]).reshape(L, O, Q, N, P)
    wrc = w_r.transpose(1, 0, 4, 2, 3).reshape(O, L, P, Q * N)
    wic = w_i.transpose(1, 0, 4, 2, 3).reshape(O, L, P, Q * N)
    ur, ui = powers(1 + jnp.arange(L, dtype=I32))
    vr_, vi_ = c_times(ur, ui)
    vrc = vr_.reshape(L, O, Q, P, N).transpose(1, 4, 0, 2, 3).reshape(O, N, L * LANES)
    vic = -vi_.reshape(L, O, Q, P, N).transpose(1, 4, 0, 2, 3).reshape(O, N, L * LANES)
    pr_l, pi_l = ur[L - 1], ui[L - 1]
    avec = jnp.stack([pr_l.reshape(O, Q * N), pi_l.reshape(O, Q * N)], axis=1)
    return kc, wrc, wic, vrc, vic, avec


def _gelu_tanh(x):
    return 0.5 * x * (1.0 + jnp.tanh(math.sqrt(2.0 / math.pi) * (x + 0.044715 * (x * x * x))))


def _ssm_body(u3_hbm, kc_ref, wrc_ref, wic_ref, vrc_ref, vic_ref, a_ref, y3_hbm,
              ubuf, ucat, er_ref, ei_ref, xr_ref, xi_ref, e2_ref, wr_ref, wi_ref, vr_ref, vi_ref, sem):
    o = pl.program_id(0)
    L, P, N, Q = SSM_L, SSM_P, SSM_N, SSM_OCT
    NC, W = er_ref.shape
    cols = pl.ds(pl.multiple_of(o * LANES, LANES), LANES)
    loads = [pltpu.make_async_copy(u3_hbm.at[:, t, cols], ubuf.at[t], sem.at[0]) for t in range(L)]
    for cp in loads:
        cp.start()

    def diag(shape, row_shift, col_shift):
        r = lax.broadcasted_iota(I32, shape, 0) >> row_shift
        c = lax.broadcasted_iota(I32, shape, 1) >> col_shift
        return r == c

    lp, ln = P.bit_length() - 1, N.bit_length() - 1
    m_k = diag((LANES, LANES), lp, lp)
    m_w = diag((LANES, Q * N), lp, ln)
    for s in range(L):
        blk = slice(s * LANES, (s + 1) * LANES)
        right = jnp.where(m_k, jnp.tile(kc_ref[s], (Q, 1)), 0.0)
        left = jnp.where(m_k, jnp.tile(kc_ref[s + 1], (Q, 1)), 0.0) if s + 1 < L else jnp.zeros_like(right)
        e2_ref[blk, :] = jnp.concatenate([left, right], axis=1).astype(BF16)
        wr_ref[blk, :] = jnp.where(m_w, jnp.tile(wrc_ref[s], (Q, 1)), 0.0).astype(BF16)
        wi_ref[blk, :] = jnp.where(m_w, jnp.tile(wic_ref[s], (Q, 1)), 0.0).astype(BF16)
    m_v = diag((Q * N, LANES), ln, lp)
    for t in range(L):
        cb = slice(t * LANES, (t + 1) * LANES)
        vr_ref[:, cb] = jnp.where(m_v, jnp.tile(vrc_ref[:, cb], (Q, 1)), 0.0).astype(BF16)
        vi_ref[:, cb] = jnp.where(m_v, jnp.tile(vic_ref[:, cb], (Q, 1)), 0.0).astype(BF16)
    for cp in loads:
        cp.wait()
    for t in range(L):
        ucat[:, t * LANES:(t + 1) * LANES] = ubuf[t].astype(BF16)
    er_ref[...] = jnp.dot(ucat[...], wr_ref[...], preferred_element_type=F32)
    ei_ref[...] = jnp.dot(ucat[...], wi_ref[...], preferred_element_type=F32)
    ar, ai = a_ref[0:1, :], a_ref[1:2, :]

    def slab(j, carry):
        cr, ci = carry
        rows = pl.ds(pl.multiple_of(j * 8, 8), 8)
        e_r, e_i = er_ref[rows, :], ei_ref[rows, :]
        for s in range(8):
            xr_ref[pl.ds(j * 8 + s, 1), :] = cr
            xi_ref[pl.ds(j * 8 + s, 1), :] = ci
            cr, ci = (ar * cr - ai * ci + e_r[s:s + 1, :], ar * ci + ai * cr + e_i[s:s + 1, :])
        return cr, ci

    lax.fori_loop(0, NC // 8, slab, (jnp.zeros((1, W), F32), jnp.zeros((1, W), F32)))
    xrb = xr_ref[...].astype(BF16)
    xib = xi_ref[...].astype(BF16)
    for a in range(L // 2):
        k = (2 * a + 2) * LANES
        y = jnp.dot(ucat[:, :k], e2_ref[(L - 2 - 2 * a) * LANES:, :], preferred_element_type=F32)
        y = y + jnp.dot(xrb, vr_ref[:, 2 * a * LANES:(2 * a + 2) * LANES], preferred_element_type=F32)
        y = y + jnp.dot(xib, vi_ref[:, 2 * a * LANES:(2 * a + 2) * LANES], preferred_element_type=F32)
        y = _gelu_tanh(y)
        ubuf[2 * a] = y[:, :LANES]
        ubuf[2 * a + 1] = y[:, LANES:]
    stores = [pltpu.make_async_copy(ubuf.at[t], y3_hbm.at[:, t, cols], sem.at[1]) for t in range(L)]
    for cp in stores:
        cp.start()
    for cp in stores:
        cp.wait()


def _ssm(u, tables):
    S, Wd = u.shape
    L, N, Q = SSM_L, SSM_N, SSM_OCT
    NC = S // L
    O = Wd // LANES
    kc, wrc, wic, vrc, vic, avec = tables
    tab = lambda a: pl.BlockSpec((None,) + a.shape[1:], lambda o: (o,) + (0,) * (a.ndim - 1))
    y3 = pl.pallas_call(
        _ssm_body,
        grid=(O,),
        in_specs=[pl.BlockSpec(memory_space=pl.ANY)] + [tab(a) for a in (kc, wrc, wic, vrc, vic, avec)],
        out_specs=pl.BlockSpec(memory_space=pl.ANY),
        out_shape=jax.ShapeDtypeStruct((NC, L, Wd), F32),
        scratch_shapes=[pltpu.VMEM((L, NC, LANES), F32), pltpu.VMEM((NC, L * LANES), BF16)]
        + [pltpu.VMEM((NC, Q * N), F32)] * 4
        + [pltpu.VMEM((L * LANES, 2 * LANES), BF16), pltpu.VMEM((L * LANES, Q * N), BF16),
           pltpu.VMEM((L * LANES, Q * N), BF16), pltpu.VMEM((Q * N, L * LANES), BF16),
           pltpu.VMEM((Q * N, L * LANES), BF16), pltpu.SemaphoreType.DMA((2,))],
        compiler_params=_cparams(1),
        name="ssm",
    )(u.reshape(NC, L, Wd), kc, wrc, wic, vrc, vic, avec)
    return y3.reshape(S, Wd)


def _mixout_body(x_ref, o1_ref, o2_ref, o3_ref, l1_ref, l2_ref, l3_ref, ys_ref,
                 wglu_ref, bglu_ref, ag_ref, sg_ref, wout_ref, mod_ref, n2g_ref, wr_ref, br_ref,
                 x1_ref, h2_ref, ri_ref, rg_ref, cnt_ref, carry_ref, *, n_heads):
    i = pl.program_id(0)
    TM = x_ref.shape[0]

    @pl.when(i == 0)
    def _():
        carry_ref[...] = jnp.zeros_like(carry_ref)

    l1, l2, l3 = l1_ref[...], l2_ref[...], l3_ref[...]
    lm = jnp.maximum(jnp.maximum(l1, l2), l3)
    e1, e2, e3 = jnp.exp(l1 - lm), jnp.exp(l2 - lm), jnp.exp(l3 - lm)
    inv = 1.0 / (e1 + e2 + e3)
    w1, w2, w3 = e1 * inv, e2 * inv, e3 * inv
    o1, o2, o3 = [jnp.concatenate(_unpack_bf16_pair(ref[...]), axis=1) for ref in (o1_ref, o2_ref, o3_ref)]
    pieces = []
    for h in range(n_heads):
        hs = slice(h * HEAD_DIM, (h + 1) * HEAD_DIM)
        pieces.append(w1[:, h:h + 1] * o1[:, hs] + w2[:, h:h + 1] * o2[:, hs] + w3[:, h:h + 1] * o3[:, hs])
    ya = _rms(jnp.concatenate(pieces, axis=1), ag_ref[...])
    ys = ys_ref[...]
    z = jnp.dot(ys.astype(BF16), wglu_ref[...], preferred_element_type=F32) + bglu_ref[...]
    yg = _rms(ys * jax.nn.sigmoid(z), sg_ref[...])
    ymix = jnp.concatenate([ya, yg], axis=1).astype(BF16)
    x1 = x_ref[...] + mod_ref[2:3, :] * jnp.dot(ymix, wout_ref[...], preferred_element_type=F32)
    x1_ref[...] = x1
    h2 = _rms(x1, n2g_ref[...]) * (1.0 + mod_ref[4:5, :]) + mod_ref[3:4, :]
    half = h2.shape[1] // 2
    h2_ref[...] = _pack_bf16_pair(h2[:, :half], h2[:, half:])
    logits = jnp.dot(h2.astype(BF16), wr_ref[...], preferred_element_type=F32) + br_ref[...]
    lane = lax.broadcasted_iota(I32, (TM, LANES), 1).astype(F32)
    work = logits
    vals, idxs = [], []
    for _ in range(TOP_K):
        m = jnp.max(work, axis=1, keepdims=True)
        idx = jnp.min(jnp.where(work == m, lane, float(LANES)), axis=1, keepdims=True)
        vals.append(m)
        idxs.append(idx)
        work = jnp.where(lane == idx, -3e38, work)
    ex = [jnp.exp(v - vals[0]) for v in vals]
    den = ex[0] + ex[1] + ex[2] + ex[3]
    onehot = jnp.zeros((TM, LANES), F32)
    for idx in idxs:
        onehot = onehot + jnp.where(lane == idx, 1.0, 0.0)
    r_i = lax.broadcasted_iota(I32, (TM, TM), 0)
    c_i = lax.broadcasted_iota(I32, (TM, TM), 1)
    tri = jnp.where(c_i < r_i, 1.0, 0.0).astype(BF16)
    before = jnp.dot(tri, onehot.astype(BF16), preferred_element_type=F32) + carry_ref[...]
    ri = jnp.zeros((TM, LANES), F32)
    rg = jnp.zeros((TM, LANES), F32)
    for k in range(TOP_K):
        rank = jnp.sum(jnp.where(lane == idxs[k], before, 0.0), axis=1, keepdims=True)
        ri = jnp.where(lane == float(k), idxs[k], ri)
        ri = jnp.where(lane == float(TOP_K + k), rank, ri)
        rg = jnp.where(lane == float(k), ex[k] / den, rg)
    ri_ref[...] = ri.astype(I32)
    rg_ref[...] = rg
    carry_ref[...] = carry_ref[...] + jnp.sum(onehot, axis=0, keepdims=True)
    cnt_ref[...] = carry_ref[...]


def _mixout(x, o, lse, ys, wglu, bglu, ag, sg, wout, mod, n2g, wr_pad, br_pad, tm):
    S, D = x.shape
    A = 2 * o[0].shape[1]
    Wd = ys.shape[1]
    row = lambda w: pl.BlockSpec((tm, w), lambda i: (i, 0))
    full = lambda a: pl.BlockSpec(a.shape, lambda i: (0,) * a.ndim)
    ins = [x, o[0], o[1], o[2], lse[0], lse[1], lse[2], ys,
           wglu, bglu, ag, sg, wout, mod, n2g, wr_pad, br_pad]
    in_specs = [row(D), row(A // 2), row(A // 2), row(A // 2), row(LANES), row(LANES), row(LANES), row(Wd)]
    in_specs += [full(a) for a in ins[8:]]
    return pl.pallas_call(
        functools.partial(_mixout_body, n_heads=A // HEAD_DIM),
        grid=(S // tm,),
        in_specs=in_specs,
        out_specs=[row(D), row(D // 2), row(LANES), row(LANES), pl.BlockSpec((1, LANES), lambda i: (0, 0))],
        out_shape=[jax.ShapeDtypeStruct((S, D), F32), jax.ShapeDtypeStruct((S, D // 2), jnp.uint32),
                   jax.ShapeDtypeStruct((S, LANES), I32), jax.ShapeDtypeStruct((S, LANES), F32),
                   jax.ShapeDtypeStruct((1, LANES), F32)],
        scratch_shapes=[pltpu.VMEM((1, LANES), F32)],
        compiler_params=_cparams(1),
        name="mixout",
    )(*ins)


def _ffn_body(ie_ref, ns_ref, tok_hbm, h_hbm, wgu_ref, bgu_ref, wd_ref, bd_ref, o_ref,
              idx_smem, xp_ref, xb_ref, sem, *, ts, tsp, sub, nf, n_items):
    i = pl.program_id(0)
    j = pl.program_id(1)
    nsub = ns_ref[i]
    slot = lax.rem(i, 2)
    nxt = jnp.minimum(i + 1, n_items - 1)
    nsub_nxt = jnp.where(i + 1 < n_items, ns_ref[nxt], 0)

    def fetch_rows(item, item_nsub, sl):
        base = pl.multiple_of(sl * tsp, tsp)
        cp = pltpu.make_async_copy(tok_hbm.at[pl.ds(pl.multiple_of(item * tsp, tsp), tsp)],
                                   idx_smem.at[pl.ds(base, tsp)], sem.at[2])
        cp.start()
        cp.wait()

        def issue8(r8, c):
            for k in range(8):
                r = r8 * 8 + k
                pltpu.make_async_copy(h_hbm.at[idx_smem[base + r]], xp_ref.at[sl, r], sem.at[sl]).start()
            return c

        lax.fori_loop(0, item_nsub * (sub // 8), issue8, 0)

    @pl.when(j == 0)
    def _():
        o_ref[...] = jnp.broadcast_to(bd_ref[...], o_ref.shape)

    @pl.when((i == 0) & (j == 0))
    def _():
        fetch_rows(0, nsub, 0)

    @pl.when((nsub > 0) & (j == 0))
    def _():
        def drain(s, c):
            pltpu.make_async_copy(h_hbm.at[pl.ds(0, sub)], xp_ref.at[slot, pl.ds(0, sub)], sem.at[slot]).wait()
            return c

        lax.fori_loop(0, nsub, drain, 0)

        def unpack(s, c):
            rows = pl.ds(pl.multiple_of(s * sub, sub), sub)
            lo, hi = _unpack_bf16_pair(xp_ref[slot, rows, :])
            xb_ref[rows, :] = jnp.concatenate([lo, hi], axis=1).astype(BF16)
            return c

        lax.fori_loop(0, nsub, unpack, 0)

    @pl.when((j == 1) & (nsub_nxt > 0))
    def _():
        fetch_rows(nxt, nsub_nxt, 1 - slot)

    def chunk_step(m):
        wgu = wgu_ref[...].astype(BF16)
        fc = wd_ref.shape[0]
        h = fc // 2
        wd = pltpu.bitcast(_pack_bf16_pair(wd_ref[:h, :], wd_ref[h:, :]), BF16)
        gu = jnp.dot(xb_ref[:m, :], wgu, preferred_element_type=F32) + bgu_ref[...]
        glu = jnp.minimum(gu, SWIGLU_LIMIT)
        a = glu * jax.nn.sigmoid(SWIGLU_ALPHA * glu)
        b = jnp.clip(gu, -SWIGLU_LIMIT, SWIGLU_LIMIT) + 1.0
        prod = a * pltpu.roll(b, 2 * fc - 1, axis=1)
        even = (lax.broadcasted_iota(I32, (m, fc), 1) & 1) == 0
        act = jnp.where(even, prod[:, :fc], pltpu.roll(prod[:, fc:], 1, axis=1)).astype(BF16)
        o_ref[:m, :] = o_ref[:m, :] + jnp.dot(act, wd, preferred_element_type=F32)

    for n in range(1, ts // sub + 1):
        @pl.when(nsub == n)
        def _(n=n):
            chunk_step(n * sub)


def _ffn(item_e, item_nsub, row_tok, h2p, wgu, bgu, wd, bd, ts, tsp, sub, fc):
    E, D, F2 = wgu.shape
    F = F2 // 2
    nf = F // fc
    n_items = item_e.shape[0]

    def chunk(i, j, ie, ns):
        return jnp.where(ns[i] > 0, j, nf - 1)

    grid_spec = pltpu.PrefetchScalarGridSpec(
        num_scalar_prefetch=2,
        grid=(n_items, nf),
        in_specs=[pl.BlockSpec(memory_space=pl.ANY),
                  pl.BlockSpec(memory_space=pl.ANY),
                  pl.BlockSpec((None, D, 2 * fc), lambda i, j, ie, ns: (ie[i], 0, chunk(i, j, ie, ns))),
                  pl.BlockSpec((None, 1, 2 * fc), lambda i, j, ie, ns: (ie[i], 0, chunk(i, j, ie, ns))),
                  pl.BlockSpec((None, fc, D), lambda i, j, ie, ns: (ie[i], chunk(i, j, ie, ns), 0)),
                  pl.BlockSpec((None, 1, D), lambda i, j, ie, ns: (ie[i], 0, 0))],
        out_specs=pl.BlockSpec((ts, D), lambda i, j, ie, ns: (i, 0)),
        scratch_shapes=[pltpu.SMEM((2 * tsp,), I32),
                        pltpu.VMEM((2, ts, D // 2), jnp.uint32),
                        pltpu.VMEM((ts, D), BF16),
                        pltpu.SemaphoreType.DMA((3,))],
    )
    return pl.pallas_call(
        functools.partial(_ffn_body, ts=ts, tsp=tsp, sub=sub, nf=nf, n_items=n_items),
        grid_spec=grid_spec,
        out_shape=jax.ShapeDtypeStruct((n_items * ts, D), F32),
        compiler_params=_cparams(2, FFN_VMEM_LIMIT),
        name="ffn",
    )(item_e, item_nsub, row_tok, h2p, wgu, bgu.reshape(E, 1, F2), wd, bd.reshape(E, 1, D))


def _combine_body(dest_hbm, ys_hbm, x1_ref, rg_ref, mod_ref, fg_ref, o_ref, idx_smem, buf_ref, sem, *, tm):
    i = pl.program_id(0)
    n = TOP_K * tm
    slot = lax.rem(i, 2)

    def fetch_rows(tile, sl):
        base = pl.multiple_of(sl * n, n)
        cp = pltpu.make_async_copy(dest_hbm.at[pl.ds(pl.multiple_of(tile * n, n), n)],
                                   idx_smem.at[pl.ds(base, n)], sem.at[2])
        cp.start()
        cp.wait()

        def issue8(r8, c):
            for k in range(8):
                r = r8 * 8 + k
                pltpu.make_async_copy(ys_hbm.at[idx_smem[base + r]], buf_ref.at[sl, r], sem.at[sl]).start()
            return c

        lax.fori_loop(0, n // 8, issue8, 0)

    @pl.when(i == 0)
    def _():
        fetch_rows(0, 0)

    @pl.when(i + 1 < pl.num_programs(0))
    def _():
        fetch_rows(i + 1, 1 - slot)

    pltpu.make_async_copy(ys_hbm.at[pl.ds(0, n)], buf_ref.at[slot], sem.at[slot]).wait()
    rg = rg_ref[...]
    y = jnp.zeros(x1_ref.shape, F32)
    for k in range(TOP_K):
        y = y + rg[:, k:k + 1] * buf_ref[slot, pl.ds(k * tm, tm), :]
    x2 = x1_ref[...] + mod_ref[5:6, :] * y
    o_ref[...] = _rms(x2, fg_ref[...])


def _combine(dest_km, y_sorted, x1, rg, mod, fg, tm):
    S, D = x1.shape
    return pl.pallas_call(
        functools.partial(_combine_body, tm=tm),
        grid=(S // tm,),
        in_specs=[pl.BlockSpec(memory_space=pl.ANY),
                  pl.BlockSpec(memory_space=pl.ANY),
                  pl.BlockSpec((tm, D), lambda i: (i, 0)),
                  pl.BlockSpec((tm, LANES), lambda i: (i, 0)),
                  pl.BlockSpec((6, D), lambda i: (0, 0)),
                  pl.BlockSpec((1, D), lambda i: (0, 0))],
        out_specs=pl.BlockSpec((tm, D), lambda i: (i, 0)),
        out_shape=jax.ShapeDtypeStruct((S, D), F32),
        scratch_shapes=[pltpu.SMEM((2 * TOP_K * tm,), I32),
                        pltpu.VMEM((2, TOP_K * tm, D), F32),
                        pltpu.SemaphoreType.DMA((3,))],
        compiler_params=_cparams(1),
        name="combine",
    )(dest_km, y_sorted, x1, rg, mod, fg.reshape(1, D))


def _routing_tables(ri, counts_f, n_experts, ts, tsp, sub, tm_c):
    S = ri.shape[0]
    counts = counts_f[0, :n_experts].astype(I32)
    nblk = (counts + ts - 1) // ts
    blk_end = jnp.cumsum(nblk)
    blk_start = blk_end - nblk
    e_idx = ri[:, :TOP_K]
    rank = ri[:, TOP_K:2 * TOP_K]
    dest = blk_start[e_idx] * ts + rank
    max_items = -(-(S * TOP_K) // ts) + n_experts
    tok = jnp.broadcast_to(jnp.arange(S, dtype=I32)[:, None], (S, TOP_K))
    slot = (dest // ts) * tsp + dest % ts
    row_tok = jnp.zeros((max_items * tsp,), I32).at[slot.reshape(-1)].set(
        tok.reshape(-1), unique_indices=True, mode='promise_in_bounds')
    item = jnp.arange(max_items, dtype=I32)
    n_items = blk_end[-1]
    item_c = jnp.minimum(item, n_items - 1)
    item_e = jnp.minimum(jnp.searchsorted(blk_end, item_c, side='right'), n_experts - 1).astype(I32)
    rows_left = counts[item_e] - (item_c - blk_start[item_e]) * ts
    nsub = jnp.clip((rows_left + sub - 1) // sub, 0, ts // sub)
    item_nsub = jnp.where(item < n_items, nsub, 0).astype(I32)
    dest_km = dest.reshape(S // tm_c, tm_c, TOP_K).transpose(0, 2, 1).reshape(-1)
    return item_e, item_nsub, row_tok, dest_km


def _forward(x, c, w_ada, b_ada, norm1_g, w_in, lambda_re, lambda_im, ssm_b_re, ssm_b_im,
             ssm_c_re, ssm_c_im, ssm_d, ssm_log_dt, w_glu, b_glu, attn_out_g, ssm_out_g,
             w_out, norm2_g, w_router, b_router, w_gate_up, b_gate_up, w_down, b_down, final_g,
             *, tiles):
    B, S, D = x.shape
    assert B == 1 and w_ada.shape[0] == 1
    A = attn_out_g.shape[1]
    Wd = ssm_out_g.shape[1]
    assert A == Wd and w_in.shape[2] == 3 * A + Wd
    E = w_router.shape[2]
    xs = x.reshape(S, D)

    mod = _adaln(c, w_ada[0], b_ada[0]).reshape(6, D)
    w_in_b = w_in[0].astype(BF16)
    qkv = _inproj(xs, norm1_g[0], mod, w_in_b[:, :3 * A], tiles['tm_in'], A, True, "inproj_qkv")
    u = _inproj(xs, norm1_g[0], mod, w_in_b[:, 3 * A:], tiles['tm_in'], Wd, False, "inproj_u")
    branches = [_attn_branch(qkv, d, A) for d in ATTN_DILATIONS]
    o = [b[0] for b in branches]
    lse = [b[1] for b in branches]

    tables = _ssm_tables(lambda_re[0], lambda_im[0], ssm_b_re[0], ssm_b_im[0], ssm_c_re[0],
                         ssm_c_im[0], ssm_d[0], ssm_log_dt[0])
    ys = _ssm(u, tables)

    wr_pad = jnp.zeros((D, LANES), BF16).at[:, :E].set(w_router[0].astype(BF16))
    br_pad = jnp.full((1, LANES), NEG_BIG, F32).at[0, :E].set(b_router[0])
    x1, h2, ri, rg, counts = _mixout(
        xs, o, lse, ys, w_glu[0].astype(BF16), b_glu[0].reshape(1, Wd), attn_out_g[0].reshape(1, A),
        ssm_out_g[0].reshape(1, Wd), w_out[0].astype(BF16), mod, norm2_g[0].reshape(1, D),
        wr_pad, br_pad, tiles['tm_mix'])

    ts, sub, fc, tm_c = tiles['ts'], tiles['sub'], tiles['fc'], tiles['tm_c']
    tsp = -(-ts // SMEM_I32_TILE) * SMEM_I32_TILE
    item_e, item_nsub, row_tok, dest_km = _routing_tables(ri, counts, E, ts, tsp, sub, tm_c)
    y_sorted = _ffn(item_e, item_nsub, row_tok, h2, w_gate_up[0], b_gate_up[0], w_down[0], b_down[0],
                    ts, tsp, sub, fc)
    out = _combine(dest_km, y_sorted, x1, rg, mod, final_g, tm_c)
    return out.reshape(B, S, D)


TILES = dict(tm_in=512, tm_mix=256, ts=1280, sub=256, fc=256, tm_c=256)


def kernel(x, c, w_ada, b_ada, norm1_g, w_in, lambda_re, lambda_im, ssm_b_re, ssm_b_im, ssm_c_re, ssm_c_im, ssm_d, ssm_log_dt, w_glu, b_glu, attn_out_g, ssm_out_g, w_out, norm2_g, w_router, b_router, w_gate_up, b_gate_up, w_down, b_down, final_g):
    return _forward(x, c, w_ada, b_ada, norm1_g, w_in, lambda_re, lambda_im, ssm_b_re, ssm_b_im,
                    ssm_c_re, ssm_c_im, ssm_d, ssm_log_dt, w_glu, b_glu, attn_out_g, ssm_out_g,
                    w_out, norm2_g, w_router, b_router, w_gate_up, b_gate_up, w_down, b_down, final_g,
                    tiles=TILES)
```
